```python
import math
import jax, jax.numpy as jnp
from jax import lax
import numpy as np

D_MODEL = 2048
BATCH = 4
SEQ = 2048
DEPTH = 2
DEC_BATCH = 128
DEC_SEQ = 8
PAST_LEN = 2048
PAGE_SIZE = 128

HEAD_DIM = 128
N_HEADS = D_MODEL // (2 * HEAD_DIM)
QK_DIM = 2 * HEAD_DIM
V_DIM = 2 * HEAD_DIM
QK_WIDTH = N_HEADS * QK_DIM
ATTN_WIDTH = N_HEADS * V_DIM
Q_BLOCK = 128
D_RNN = D_MODEL
N_RNN_BLOCKS = 8
RNN_BLOCK = D_RNN // N_RNN_BLOCKS
CONV_W = 4
RG_C = 8.0
N_EXPERTS = 16
N_GROUPS = 4
EXPERTS_PER_GROUP = N_EXPERTS // N_GROUPS
TOP_K = 2
D_EXPERT = 1024
PLE_DIM = 256
LN_EPS = 1e-5
IN_SPLITS = [QK_WIDTH, 2 * QK_WIDTH, 2 * QK_WIDTH + ATTN_WIDTH,
             2 * QK_WIDTH + ATTN_WIDTH + D_RNN, 2 * QK_WIDTH + ATTN_WIDTH + 2 * D_RNN,
             2 * QK_WIDTH + ATTN_WIDTH + 2 * D_RNN + D_MODEL]
N_IN = 2 * QK_WIDTH + ATTN_WIDTH + 2 * D_RNN + 2 * D_MODEL

kernel_name = 'hybrid_diffattn_rglru_grouped_moe_step'


def layer_norm(x, g, b):
    xf = x.astype(jnp.float32)
    mu = xf.mean(-1, keepdims=True)
    var = jnp.square(xf - mu).mean(-1, keepdims=True)
    return ((xf - mu) * lax.rsqrt(var + LN_EPS) * g.astype(jnp.float32) + b.astype(jnp.float32)).astype(x.dtype)


def rms_norm(x, g):
    xf = x.astype(jnp.float32)
    ms = jnp.square(xf).mean(-1, keepdims=True)
    return (xf * lax.rsqrt(ms + LN_EPS) * g.astype(jnp.float32)).astype(x.dtype)


def diff_lambda(lq1, lk1, lq2, lk2, lam_init):
    f = jnp.float32
    return (jnp.exp(jnp.sum(lq1.astype(f) * lk1.astype(f)))
            - jnp.exp(jnp.sum(lq2.astype(f) * lk2.astype(f))) + lam_init)


def diff_attn_prompt(q, k, v, lam):
    B, S = q.shape[:2]
    n_blocks = S // Q_BLOCK
    k5 = k.reshape(B, S, N_HEADS, 2, HEAD_DIM)
    q_blocks = q.reshape(B, n_blocks, Q_BLOCK, N_HEADS, 2, HEAD_DIM).swapaxes(0, 1)
    k_pos = jnp.arange(S)
    scale = HEAD_DIM ** -0.5

    def one_block(args):
        blk, qb = args
        s = jnp.einsum('bqhcd,bkhcd->bchqk', qb, k5).astype(jnp.float32) * scale
        q_pos = blk * Q_BLOCK + jnp.arange(Q_BLOCK)
        causal = k_pos[None, :] <= q_pos[:, None]
        p = jax.nn.softmax(jnp.where(causal, s, -jnp.inf), axis=-1)
        w = (p[:, 0] - lam * p[:, 1]).astype(v.dtype)
        return jnp.einsum('bhqk,bkhd->bqhd', w, v)

    o = lax.map(one_block, (jnp.arange(n_blocks), q_blocks))
    return o.swapaxes(0, 1).reshape(B, S, N_HEADS, V_DIM)


def online_softmax_update(carry, s, v):
    m, l, acc = carry
    m_new = jnp.maximum(m, s.max(-1))
    corr = jnp.exp(m - m_new)
    p = jnp.exp(s - m_new[..., None])
    l = l * corr + p.sum(-1)
    acc = acc * corr[..., None] + jnp.einsum('bchqk,bkhd->bchqd', p, v.astype(jnp.float32))
    return (m_new, l, acc)


def diff_attn_sample(q, k, v, cache_k, cache_v, page_table, layer, lam):
    Bd, Sd = q.shape[:2]
    scale = HEAD_DIM ** -0.5
    f = jnp.float32
    qf = q.astype(f)
    init = (jnp.full((Bd, 2, N_HEADS, Sd), -1e30, f),
            jnp.zeros((Bd, 2, N_HEADS, Sd), f),
            jnp.zeros((Bd, 2, N_HEADS, Sd, V_DIM), f))

    def page_step(carry, pages):
        kb = cache_k[layer, pages].reshape(Bd, PAGE_SIZE, N_HEADS, 2, HEAD_DIM)
        vb = cache_v[layer, pages]
        s = jnp.einsum('bqhcd,bkhcd->bchqk', qf, kb.astype(f)) * scale
        return online_softmax_update(carry, s, vb), None

    carry, _ = lax.scan(page_step, init, page_table.T)
    k5 = k.reshape(Bd, Sd, N_HEADS, 2, HEAD_DIM)
    s = jnp.einsum('bqhcd,bkhcd->bchqk', qf, k5.astype(f)) * scale
    causal = jnp.tril(jnp.ones((Sd, Sd), dtype=bool))
    s = jnp.where(causal, s, -jnp.inf)
    m, l, acc = online_softmax_update(carry, s, v)
    o = acc / l[..., None]
    o = o[:, 0] - lam * o[:, 1]
    return o.transpose(0, 2, 1, 3).astype(v.dtype)


def rglru_branch(xr, conv_buf, h0, conv_w, conv_b, w_a, b_a, w_x, b_x, lam):
    B, S, _ = xr.shape
    f = jnp.float32
    xp = jnp.concatenate([conv_buf.astype(xr.dtype), xr], axis=1)
    xc = conv_b + sum(xp[:, j:j + S] * conv_w[j] for j in range(CONV_W))
    new_buf = xp[:, S:]
    xb = xc.reshape(B, S, N_RNN_BLOCKS, RNN_BLOCK)
    r = jax.nn.sigmoid(jnp.einsum('bsnc,ncd->bsnd', xb, w_a).reshape(B, S, D_RNN) + b_a)
    i = jax.nn.sigmoid(jnp.einsum('bsnc,ncd->bsnd', xb, w_x).reshape(B, S, D_RNN) + b_x)
    log_a = -RG_C * r.astype(f) * jax.nn.softplus(-lam.astype(f))
    a = jnp.exp(log_a)
    u = jnp.sqrt(-jnp.expm1(2.0 * log_a)) * (i * xc).astype(f)

    def step(h, au):
        a_t, u_t = au
        h = a_t * h + u_t
        return h, h

    h_last, hs = lax.scan(step, h0.astype(f), (a.swapaxes(0, 1), u.swapaxes(0, 1)))
    return hs.swapaxes(0, 1).astype(xr.dtype), new_buf, h_last


def grouped_moe(h, router_w, router_b, w_gate, w_up, w_down):
    B, S, D = h.shape
    t = h.reshape(B * S, D)
    T = t.shape[0]
    logits = (t @ router_w).astype(jnp.float32) + router_b.astype(jnp.float32)
    probs = jax.nn.softmax(logits, axis=-1)
    pg = probs.reshape(T, N_GROUPS, EXPERTS_PER_GROUP)
    group_score = lax.top_k(pg, TOP_K)[0].sum(-1)
    g_sel = jnp.argmax(group_score, axis=-1)
    in_group = jnp.take_along_axis(pg, g_sel[:, None, None], axis=1)[:, 0]
    w_top, i_top = lax.top_k(in_group, TOP_K)
    w_top = w_top / w_top.sum(-1, keepdims=True)
    expert_id = g_sel[:, None] * EXPERTS_PER_GROUP + i_top
    combine = jnp.einsum('tk,tke->te', w_top, jax.nn.one_hot(expert_id, N_EXPERTS, dtype=jnp.float32)).astype(t.dtype)
    y = jnp.zeros_like(t)
    for e in range(N_EXPERTS):
        he = jax.nn.silu(t @ w_gate[e]) * (t @ w_up[e])
        y = y + combine[:, e:e + 1] * (he @ w_down[e])
    return y.reshape(B, S, D)


def layer_step(x, p_l, conv_buf, h0, attend, lam_init, lp):
    B, S, _ = x.shape
    alpha = (2.0 * DEPTH) ** 0.25
    z = x @ lp['w_in']
    q, k, v, xr, gr, ga, gb = jnp.split(z, IN_SPLITS, axis=-1)
    q = q.reshape(B, S, N_HEADS, 2, HEAD_DIM)
    k = k.reshape(B, S, N_HEADS, QK_DIM)
    v = v.reshape(B, S, N_HEADS, V_DIM)
    o_attn = attend(q, k, v)
    o_attn = rms_norm(o_attn, lp['subln_w']) * (1.0 - lam_init)
    y_rnn, new_buf, h_last = rglru_branch(xr, conv_buf, h0, lp['conv_w'], lp['conv_b'], lp['rg_w_a'],
                                          lp['rg_b_a'], lp['rg_w_x'], lp['rg_b_x'], lp['rg_lambda'])
    y_rnn = y_rnn * jax.nn.gelu(gr)
    u_attn = o_attn.reshape(B, S, ATTN_WIDTH) @ lp['w_branch_attn']
    u_rnn = y_rnn @ lp['w_branch_rnn']
    mixed = jax.nn.sigmoid(ga) * u_attn + jax.nn.sigmoid(gb) * u_rnn
    x = layer_norm(alpha * x + mixed @ lp['w_out'], lp['ln1_g'], lp['ln1_b'])
    ple = jax.nn.sigmoid(x @ lp['w_ple_gate']) * (p_l @ lp['w_ple_proj'])
    moe_out = grouped_moe(x, lp['router_w'], lp['router_b'], lp['moe_w_gate'], lp['moe_w_up'], lp['moe_w_down'])
    x = layer_norm(alpha * x + moe_out + ple, lp['ln2_g'], lp['ln2_b'])
    return x, k, v, new_buf, h_last


def setup_inputs(seed: int = 0) -> dict:
    key = jax.random.key(seed)
    ks = iter(jax.random.split(key, 48))
    f = jnp.float32
    beta = (8.0 * DEPTH) ** -0.25
    n_pages = PAST_LEN // PAGE_SIZE
    n_used = DEC_BATCH * n_pages
    n_pool = n_used + n_used // 4

    def nrm(shape, scale=1.0):
        return jax.random.normal(next(ks), shape, f) * scale

    inp = {}
    inp['x_prompt'] = nrm((BATCH, SEQ, D_MODEL))
    inp['x_sample'] = nrm((DEC_BATCH, DEC_SEQ, D_MODEL))
    inp['cache_k'] = nrm((DEPTH, n_pool, PAGE_SIZE, N_HEADS, QK_DIM))
    inp['cache_v'] = nrm((DEPTH, n_pool, PAGE_SIZE, N_HEADS, V_DIM))
    inp['state_conv'] = nrm((DEPTH, DEC_BATCH, CONV_W - 1, D_RNN))
    inp['state_h'] = nrm((DEPTH, DEC_BATCH, D_RNN), 0.5)
    inp['page_table'] = jax.random.permutation(next(ks), n_pool)[:n_used].reshape(DEC_BATCH, n_pages).astype(jnp.int32)
    inp['p_prompt'] = nrm((DEPTH, BATCH, SEQ, PLE_DIM))
    inp['p_sample'] = nrm((DEPTH, DEC_BATCH, DEC_SEQ, PLE_DIM))
    inp['emb_ln_g'] = 1.0 + nrm((D_MODEL,), 0.02)
    inp['emb_ln_b'] = nrm((D_MODEL,), 0.02)
    inp['w_in'] = nrm((DEPTH, D_MODEL, N_IN), D_MODEL ** -0.5)
    inp['lambda_q1'] = nrm((DEPTH, HEAD_DIM), 0.1)
    inp['lambda_k1'] = nrm((DEPTH, HEAD_DIM), 0.1)
    inp['lambda_q2'] = nrm((DEPTH, HEAD_DIM), 0.1)
    inp['lambda_k2'] = nrm((DEPTH, HEAD_DIM), 0.1)
    inp['subln_w'] = 1.0 + nrm((DEPTH, V_DIM), 0.02)
    inp['conv_w'] = nrm((DEPTH, CONV_W, D_RNN), CONV_W ** -0.5)
    inp['conv_b'] = nrm((DEPTH, D_RNN), 0.01)
    inp['rg_w_a'] = nrm((DEPTH, N_RNN_BLOCKS, RNN_BLOCK, RNN_BLOCK), RNN_BLOCK ** -0.5)
    inp['rg_b_a'] = nrm((DEPTH, D_RNN), 0.01)
    inp['rg_w_x'] = nrm((DEPTH, N_RNN_BLOCKS, RNN_BLOCK, RNN_BLOCK), RNN_BLOCK ** -0.5)
    inp['rg_b_x'] = nrm((DEPTH, D_RNN), 0.01)
    a_c = jax.random.uniform(next(ks), (DEPTH, D_RNN), f, minval=0.9, maxval=0.999)
    s = a_c ** (1.0 / RG_C)
    inp['rg_lambda'] = jnp.log(s) - jnp.log1p(-s)
    inp['w_branch_attn'] = nrm((DEPTH, ATTN_WIDTH, D_MODEL), ATTN_WIDTH ** -0.5)
    inp['w_branch_rnn'] = nrm((DEPTH, D_RNN, D_MODEL), D_RNN ** -0.5)
    inp['w_out'] = nrm((DEPTH, D_MODEL, D_MODEL), beta * D_MODEL ** -0.5)
    inp['ln1_g'] = 1.0 + nrm((DEPTH, D_MODEL), 0.02)
    inp['ln1_b'] = nrm((DEPTH, D_MODEL), 0.02)
    inp['router_w'] = nrm((D_MODEL, N_EXPERTS), D_MODEL ** -0.5)
    inp['router_b'] = nrm((N_EXPERTS,), 0.01)
    inp['moe_w_gate'] = nrm((DEPTH, N_EXPERTS, D_MODEL, D_EXPERT), D_MODEL ** -0.5)
    inp['moe_w_up'] = nrm((DEPTH, N_EXPERTS, D_MODEL, D_EXPERT), D_MODEL ** -0.5)
    inp['moe_w_down'] = nrm((DEPTH, N_EXPERTS, D_EXPERT, D_MODEL), beta * D_EXPERT ** -0.5)
    inp['w_ple_gate'] = nrm((DEPTH, D_MODEL, D_MODEL), D_MODEL ** -0.5)
    inp['w_ple_proj'] = nrm((DEPTH, PLE_DIM, D_MODEL), beta * PLE_DIM ** -0.5)
    inp['ln2_g'] = 1.0 + nrm((DEPTH, D_MODEL), 0.02)
    inp['ln2_b'] = nrm((DEPTH, D_MODEL), 0.02)
    return inp


def reference(x_prompt, x_sample, cache_k, cache_v, state_conv, state_h, page_table, p_prompt, p_sample,
              emb_ln_g, emb_ln_b, w_in, lambda_q1, lambda_k1, lambda_q2, lambda_k2, subln_w,
              conv_w, conv_b, rg_w_a, rg_b_a, rg_w_x, rg_b_x, rg_lambda,
              w_branch_attn, w_branch_rnn, w_out, ln1_g, ln1_b, router_w, router_b,
              moe_w_gate, moe_w_up, moe_w_down, w_ple_gate, w_ple_proj, ln2_g, ln2_b):
    xp = layer_norm(x_prompt, emb_ln_g, emb_ln_b)
    xs = layer_norm(x_sample, emb_ln_g, emb_ln_b)
    bp = x_prompt.shape[0]
    conv0 = jnp.zeros((bp, CONV_W - 1, D_RNN), x_prompt.dtype)
    h0 = jnp.zeros((bp, D_RNN), jnp.float32)
    kp, vp, cp, hp, ksm, vsm, csm, hsm = [], [], [], [], [], [], [], []
    for l in range(DEPTH):
        lam_init = 0.8 - 0.6 * math.exp(-0.3 * l)
        lam = diff_lambda(lambda_q1[l], lambda_k1[l], lambda_q2[l], lambda_k2[l], lam_init)
        lp = {'w_in': w_in[l], 'subln_w': subln_w[l], 'conv_w': conv_w[l], 'conv_b': conv_b[l],
              'rg_w_a': rg_w_a[l], 'rg_b_a': rg_b_a[l], 'rg_w_x': rg_w_x[l], 'rg_b_x': rg_b_x[l],
              'rg_lambda': rg_lambda[l], 'w_branch_attn': w_branch_attn[l], 'w_branch_rnn': w_branch_rnn[l],
              'w_out': w_out[l], 'ln1_g': ln1_g[l], 'ln1_b': ln1_b[l], 'router_w': router_w,
              'router_b': router_b, 'moe_w_gate': moe_w_gate[l], 'moe_w_up': moe_w_up[l],
              'moe_w_down': moe_w_down[l], 'w_ple_gate': w_ple_gate[l], 'w_ple_proj': w_ple_proj[l],
              'ln2_g': ln2_g[l], 'ln2_b': ln2_b[l]}
        attend_prompt = lambda q, k, v: diff_attn_prompt(q, k, v, lam)
        attend_sample = lambda q, k, v: diff_attn_sample(q, k, v, cache_k, cache_v, page_table, l, lam)
        xp, k_p, v_p, c_p, h_p = layer_step(xp, p_prompt[l], conv0, h0, attend_prompt, lam_init, lp)
        xs, k_s, v_s, c_s, h_s = layer_step(xs, p_sample[l], state_conv[l], state_h[l], attend_sample, lam_init, lp)
        kp.append(k_p); vp.append(v_p); cp.append(c_p); hp.append(h_p)
        ksm.append(k_s); vsm.append(v_s); csm.append(c_s); hsm.append(h_s)
    new_k_prompt = jnp.stack(kp)
    new_v_prompt = jnp.stack(vp)
    new_conv_prompt = jnp.stack(cp)
    new_h_prompt = jnp.stack(hp)
    new_k_sample = jnp.stack(ksm)
    new_v_sample = jnp.stack(vsm)
    new_conv_sample = jnp.stack(csm)
    new_h_sample = jnp.stack(hsm)
    return (xp, xs, new_k_prompt, new_v_prompt, new_conv_prompt, new_h_prompt,
            new_k_sample, new_v_sample, new_conv_sample, new_h_sample)
```

```python
import functools
import math

import jax
import jax.numpy as jnp
from jax import lax
from jax.experimental import pallas as pl
from jax.experimental.pallas import tpu as pltpu

F32 = jnp.float32
BF16 = jnp.bfloat16

D_MODEL = 2048
BATCH = 4
SEQ = 2048
DEPTH = 2
DEC_BATCH = 128
DEC_SEQ = 8
PAST_LEN = 2048
PAGE_SIZE = 128
N_PAGES = PAST_LEN // PAGE_SIZE
HEAD_DIM = 128
N_HEADS = D_MODEL // (2 * HEAD_DIM)
HEAD_W = 2 * HEAD_DIM
D_RNN = D_MODEL
N_RNN_BLOCKS = 8
RNN_BLOCK = D_RNN // N_RNN_BLOCKS
CONV_W = 4
RG_C = 8.0
N_EXPERTS = 16
N_GROUPS = 4
EXPERTS_PER_GROUP = N_EXPERTS // N_GROUPS
D_EXPERT = 1024
PLE_DIM = 256
LN_EPS = 1e-5
N_IN = 7 * D_MODEL
ALPHA = (2.0 * DEPTH) ** 0.25
ATTN_SCALE = HEAD_DIM ** -0.5

TP = BATCH * SEQ
TS = DEC_BATCH * DEC_SEQ
T = TP + TS

COL_Q, COL_K, COL_V, COL_XR, COL_GR, COL_GA, COL_GB = range(7)

SUBLANES = 8
LANES = 128
VMEM_LIMIT = 56 * 1024 * 1024
NEG_BIG = -1e30


def _cparams(sem):
    return pltpu.CompilerParams(dimension_semantics=sem, vmem_limit_bytes=VMEM_LIMIT)


def _layer_norm(x, g, b):
    mu = jnp.mean(x, axis=-1, keepdims=True)
    xc = x - mu
    var = jnp.mean(xc * xc, axis=-1, keepdims=True)
    return xc * lax.rsqrt(var + LN_EPS) * g + b


def _dot(a, b):
    return jnp.dot(a, b, preferred_element_type=F32)


def _dot_nt(a, b):
    return lax.dot_general(a, b, (((1,), (1,)), ((), ())), preferred_element_type=F32)


def _diff_lambda(lq1, lk1, lq2, lk2, lam_init):
    return (jnp.exp(jnp.sum(lq1 * lk1, axis=-1, keepdims=True))
            - jnp.exp(jnp.sum(lq2 * lk2, axis=-1, keepdims=True)) + lam_init)


EMB_TM = 512


def _embed_ln_kernel(xp_ref, xs_ref, g_ref, b_ref, o32_ref, o16_ref):
    i = pl.program_id(0)

    def go(x):
        y = _layer_norm(x, g_ref[...], b_ref[...])
        o32_ref[...] = y
        o16_ref[...] = y.astype(BF16)

    @pl.when(i < TP // EMB_TM)
    def _():
        go(xp_ref[...])

    @pl.when(i >= TP // EMB_TM)
    def _():
        go(xs_ref[...])


def _embed_ln(xp, xs, g, b):
    n_p = TP // EMB_TM
    return pl.pallas_call(
        _embed_ln_kernel,
        grid=(T // EMB_TM,),
        in_specs=[
            pl.BlockSpec((EMB_TM, D_MODEL), lambda i: (jnp.minimum(i, n_p - 1), 0)),
            pl.BlockSpec((EMB_TM, D_MODEL), lambda i: (jnp.maximum(i - n_p, 0), 0)),
            pl.BlockSpec((1, D_MODEL), lambda i: (0, 0)),
            pl.BlockSpec((1, D_MODEL), lambda i: (0, 0)),
        ],
        out_specs=[pl.BlockSpec((EMB_TM, D_MODEL), lambda i: (i, 0)),
                   pl.BlockSpec((EMB_TM, D_MODEL), lambda i: (i, 0))],
        out_shape=[jax.ShapeDtypeStruct((T, D_MODEL), F32), jax.ShapeDtypeStruct((T, D_MODEL), BF16)],
        compiler_params=_cparams(("arbitrary",)),
        name="embed_ln",
    )(xp, xs, g, b)


IN_TM = 1024
IN_TN = 1024


def _matmul_kernel(a_ref, w_ref, o_ref):
    o_ref[...] = _dot(a_ref[...], w_ref[...]).astype(o_ref.dtype)


def _in_proj(x16, w_in16, layer):
    return pl.pallas_call(
        _matmul_kernel,
        grid=(N_IN // IN_TN, T // IN_TM),
        in_specs=[
            pl.BlockSpec((IN_TM, D_MODEL), lambda j, i: (i, 0)),
            pl.BlockSpec((None, D_MODEL, IN_TN), lambda j, i: (layer, 0, j)),
        ],
        out_specs=pl.BlockSpec((IN_TM, IN_TN), lambda j, i: (i, j)),
        out_shape=jax.ShapeDtypeStruct((T, N_IN), F32),
        compiler_params=_cparams(("arbitrary", "arbitrary")),
        name="in_proj",
    )(x16, w_in16)


ATT_TQ = 512
ATT_TK = 512
ATT_NQ = SEQ // ATT_TQ


def _attn_tables():
    qi, kj = [], []
    for q in range(ATT_NQ):
        for k in range(q + 1):
            qi.append(q)
            kj.append(k)
    return jnp.asarray(qi, jnp.int32), jnp.asarray(kj, jnp.int32)


def _headnorm_store(o, subln, lam_init):
    ms = jnp.mean(o * o, axis=-1, keepdims=True)
    return o * lax.rsqrt(ms + LN_EPS) * subln * (1.0 - lam_init)


def _attn_prompt_kernel(qi_tab, kj_tab, q_ref, k_ref, v_ref, lq1, lk1, lq2, lk2, subln_ref, o_ref,
                        m_ref, l_ref, acc_ref, *, lam_init):
    s_idx = pl.program_id(2)
    qi = qi_tab[s_idx]
    kj = kj_tab[s_idx]

    @pl.when(kj == 0)
    def _():
        m_ref[...] = jnp.full(m_ref.shape, NEG_BIG, F32)
        l_ref[...] = jnp.zeros(l_ref.shape, F32)
        acc_ref[...] = jnp.zeros(acc_ref.shape, F32)

    q = q_ref[...].astype(BF16)
    k = k_ref[...].astype(BF16)
    v = v_ref[...].astype(BF16)
    row = qi * ATT_TQ + lax.broadcasted_iota(jnp.int32, (ATT_TQ, ATT_TK), 0)
    col = kj * ATT_TK + lax.broadcasted_iota(jnp.int32, (ATT_TQ, ATT_TK), 1)
    causal = col <= row
    for c in range(2):
        sl = slice(c * HEAD_DIM, (c + 1) * HEAD_DIM)
        s = _dot_nt(q[:, sl], k[:, sl]) * ATTN_SCALE
        s = jnp.where(causal, s, NEG_BIG)
        m_prev = m_ref[c]
        m_new = jnp.maximum(m_prev, jnp.max(s, axis=-1, keepdims=True))
        corr = jnp.exp(m_prev - m_new)
        p = jnp.exp(s - m_new)
        l_ref[c] = l_ref[c] * corr + jnp.sum(p, axis=-1, keepdims=True)
        acc_ref[c] = acc_ref[c] * corr + _dot(p.astype(BF16), v)
        m_ref[c] = m_new

    @pl.when(kj == qi)
    def _():
        lam = _diff_lambda(lq1[...], lk1[...], lq2[...], lk2[...], lam_init)
        o = acc_ref[0] / l_ref[0] - lam * (acc_ref[1] / l_ref[1])
        o_ref[...] = _headnorm_store(o, subln_ref[...], lam_init).astype(o_ref.dtype)


def _attn_prompt(z, lam_vecs, subln, lam_init):
    qi_tab, kj_tab = _attn_tables()
    n_steps = qi_tab.shape[0]
    vec = pl.BlockSpec((1, HEAD_DIM), lambda b, h, s, qt, kt: (0, 0))
    grid_spec = pltpu.PrefetchScalarGridSpec(
        num_scalar_prefetch=2,
        grid=(BATCH, N_HEADS, n_steps),
        in_specs=[
            pl.BlockSpec((ATT_TQ, HEAD_W), lambda b, h, s, qt, kt: (b * ATT_NQ + qt[s], COL_Q * N_HEADS + h)),
            pl.BlockSpec((ATT_TK, HEAD_W), lambda b, h, s, qt, kt: (b * ATT_NQ + kt[s], COL_K * N_HEADS + h)),
            pl.BlockSpec((ATT_TK, HEAD_W), lambda b, h, s, qt, kt: (b * ATT_NQ + kt[s], COL_V * N_HEADS + h)),
            vec, vec, vec, vec,
            pl.BlockSpec((1, HEAD_W), lambda b, h, s, qt, kt: (0, 0)),
        ],
        out_specs=pl.BlockSpec((ATT_TQ, HEAD_W), lambda b, h, s, qt, kt: (b * ATT_NQ + qt[s], h)),
        scratch_shapes=[pltpu.VMEM((2, ATT_TQ, 1), F32), pltpu.VMEM((2, ATT_TQ, 1), F32),
                        pltpu.VMEM((2, ATT_TQ, HEAD_W), F32)],
    )
    return pl.pallas_call(
        functools.partial(_attn_prompt_kernel, lam_init=lam_init),
        grid_spec=grid_spec,
        out_shape=jax.ShapeDtypeStruct((TP, D_MODEL), BF16),
        compiler_params=_cparams(("arbitrary", "arbitrary", "arbitrary")),
        name="attn_prompt",
    )(qi_tab, kj_tab, z, z, z, *lam_vecs, subln)


QROWS = N_HEADS * 2 * DEC_SEQ
HROWS = 2 * DEC_SEQ


def _attn_sample_kernel(pt_ref, q_ref, kc_ref, vc_ref, kn_ref, vn_ref, lq1, lk1, lq2, lk2, subln_ref, o_ref,
                        qb_ref, m_ref, l_ref, acc_ref, *, lam_init):
    p_idx = pl.program_id(1)

    @pl.when(p_idx == 0)
    def _():
        m_ref[...] = jnp.full(m_ref.shape, NEG_BIG, F32)
        l_ref[...] = jnp.zeros(l_ref.shape, F32)
        acc_ref[...] = jnp.zeros(acc_ref.shape, F32)
        q = q_ref[...]
        lane = lax.broadcasted_iota(jnp.int32, (DEC_SEQ, HEAD_W), 1)
        for h in range(N_HEADS):
            qh = q[:, h * HEAD_W:(h + 1) * HEAD_W]
            qb = jnp.concatenate([jnp.where(lane < HEAD_DIM, qh, 0.0), jnp.where(lane >= HEAD_DIM, qh, 0.0)], axis=0)
            qb_ref[h * HROWS:(h + 1) * HROWS, :] = qb.astype(BF16)

    def update(kb, vb, mask):
        s = jnp.concatenate(
            [_dot_nt(qb_ref[h * HROWS:(h + 1) * HROWS, :], kb[:, h * HEAD_W:(h + 1) * HEAD_W])
             for h in range(N_HEADS)], axis=0) * ATTN_SCALE
        if mask is not None:
            s = jnp.where(mask, s, NEG_BIG)
        m_prev = m_ref[...]
        m_new = jnp.maximum(m_prev, jnp.max(s, axis=-1, keepdims=True))
        corr = jnp.exp(m_prev - m_new)
        p = jnp.exp(s - m_new)
        l_ref[...] = l_ref[...] * corr + jnp.sum(p, axis=-1, keepdims=True)
        pb = p.astype(BF16)
        pv = jnp.concatenate(
            [_dot(pb[h * HROWS:(h + 1) * HROWS, :], vb[:, h * HEAD_W:(h + 1) * HEAD_W])
             for h in range(N_HEADS)], axis=0)
        acc_ref[...] = acc_ref[...] * corr + pv
        m_ref[...] = m_new

    update(kc_ref[...].astype(BF16), vc_ref[...].astype(BF16), None)

    @pl.when(p_idx == N_PAGES - 1)
    def _():
        pad = jnp.zeros((PAGE_SIZE - DEC_SEQ, D_MODEL), F32)
        kb = jnp.concatenate([kn_ref[...], pad], axis=0).astype(BF16)
        vb = jnp.concatenate([vn_ref[...], pad], axis=0).astype(BF16)
        qpos = lax.broadcasted_iota(jnp.int32, (QROWS, PAGE_SIZE), 0) % DEC_SEQ
        kpos = lax.broadcasted_iota(jnp.int32, (QROWS, PAGE_SIZE), 1)
        update(kb, vb, kpos <= qpos)
        lam = _diff_lambda(lq1[...], lk1[...], lq2[...], lk2[...], lam_init)
        o = acc_ref[...] / l_ref[...]
        for h in range(N_HEADS):
            oh = o[h * HROWS:h * HROWS + DEC_SEQ, :] - lam * o[h * HROWS + DEC_SEQ:(h + 1) * HROWS, :]
            o_ref[:, h * HEAD_W:(h + 1) * HEAD_W] = _headnorm_store(oh, subln_ref[...], lam_init)


def _attn_sample(z, cache_k, cache_v, page_table, lam_vecs, subln, lam_init, layer):
    n_pool = cache_k.shape[1]
    ck = cache_k.reshape(DEPTH, n_pool, PAGE_SIZE, D_MODEL)
    cv = cache_v.reshape(DEPTH, n_pool, PAGE_SIZE, D_MODEL)
    pt = page_table.reshape(-1)
    row0 = TP // DEC_SEQ
    vec = pl.BlockSpec((1, HEAD_DIM), lambda b, p, pt: (0, 0))
    grid_spec = pltpu.PrefetchScalarGridSpec(
        num_scalar_prefetch=1,
        grid=(DEC_BATCH, N_PAGES),
        in_specs=[
            pl.BlockSpec((DEC_SEQ, D_MODEL), lambda b, p, pt: (row0 + b, COL_Q)),
            pl.BlockSpec((None, None, PAGE_SIZE, D_MODEL), lambda b, p, pt: (layer, pt[b * N_PAGES + p], 0, 0)),
            pl.BlockSpec((None, None, PAGE_SIZE, D_MODEL), lambda b, p, pt: (layer, pt[b * N_PAGES + p], 0, 0)),
            pl.BlockSpec((DEC_SEQ, D_MODEL), lambda b, p, pt: (row0 + b, COL_K)),
            pl.BlockSpec((DEC_SEQ, D_MODEL), lambda b, p, pt: (row0 + b, COL_V)),
            vec, vec, vec, vec,
            pl.BlockSpec((1, HEAD_W), lambda b, p, pt: (0, 0)),
        ],
        out_specs=pl.BlockSpec((DEC_SEQ, D_MODEL), lambda b, p, pt: (b, 0)),
        scratch_shapes=[pltpu.VMEM((QROWS, HEAD_W), BF16), pltpu.VMEM((QROWS, 1), F32),
                        pltpu.VMEM((QROWS, 1), F32), pltpu.VMEM((QROWS, HEAD_W), F32)],
    )
    return pl.pallas_call(
        functools.partial(_attn_sample_kernel, lam_init=lam_init),
        grid_spec=grid_spec,
        out_shape=jax.ShapeDtypeStruct((TS, D_MODEL), F32),
        compiler_params=_cparams(("arbitrary", "arbitrary")),
        name="attn_sample",
    )(pt, z, ck, cv, z, z, *lam_vecs, subln)


RG_TC = 256


def _gelu_tanh(x):
    return 0.5 * x * (1.0 + jnp.tanh(math.sqrt(2.0 / math.pi) * (x + 0.044715 * (x * x * x))))


def _rglru_kernel(*refs, carry):
    if carry:
        (xr_ref, gr_ref, cw_ref, cb_ref, wa_ref, ba_ref, wx_ref, bx_ref, lam_ref,
         y_ref, hl_ref, xprev_ref, hc_ref, a_s, u_s) = refs
    else:
        (xr_ref, gr_ref, hist_ref, h0_ref, cw_ref, cb_ref, wa_ref, ba_ref, wx_ref, bx_ref, lam_ref,
         y_ref, hs_ref) = refs
    t = pl.program_id(1) if carry else None
    x = xr_ref[...]
    n = x.shape[0]
    tpos = lax.broadcasted_iota(jnp.int32, x.shape, 0) % SUBLANES

    if carry:
        @pl.when(t == 0)
        def _():
            xprev_ref[...] = jnp.zeros(xprev_ref.shape, F32)
            hc_ref[...] = jnp.zeros(hc_ref.shape, F32)

    xc = cb_ref[...] + cw_ref[CONV_W - 1:CONV_W, :] * x
    for s in range(1, CONV_W):
        w_s = cw_ref[CONV_W - 1 - s:CONV_W - s, :]
        if carry:
            xc = xc + w_s * pltpu.roll(x, s, axis=0)
        else:
            hist = pltpu.roll(hist_ref[...], n - SUBLANES + s, axis=0)
            xc = xc + w_s * jnp.where(tpos >= s, pltpu.roll(x, s, axis=0), hist)
    if carry:
        x0 = x[:SUBLANES]
        xp = xprev_ref[...]
        t0 = lax.broadcasted_iota(jnp.int32, x0.shape, 0)
        xc0 = cb_ref[...] + cw_ref[CONV_W - 1:CONV_W, :] * x0
        for s in range(1, CONV_W):
            w_s = cw_ref[CONV_W - 1 - s:CONV_W - s, :]
            xc0 = xc0 + w_s * jnp.where(t0 >= s, pltpu.roll(x0, s, axis=0), pltpu.roll(xp, s, axis=0))
        xc = jnp.concatenate([xc0, xc[SUBLANES:]], axis=0)
        xprev_ref[...] = x[n - SUBLANES:]

    xcb = xc.astype(BF16)
    ga = jnp.concatenate([_dot(xcb[:, j * RNN_BLOCK:(j + 1) * RNN_BLOCK], wa_ref[j]) for j in range(N_RNN_BLOCKS)],
                         axis=1)
    gx = jnp.concatenate([_dot(xcb[:, j * RNN_BLOCK:(j + 1) * RNN_BLOCK], wx_ref[j]) for j in range(N_RNN_BLOCKS)],
                         axis=1)
    r = jax.nn.sigmoid(ga + ba_ref[...])
    i = jax.nn.sigmoid(gx + bx_ref[...])
    nl = -lam_ref[...]
    softplus = jnp.maximum(nl, 0.0) + jnp.log1p(jnp.exp(-jnp.abs(nl)))
    log_a = (-RG_C) * r * softplus
    a = jnp.exp(log_a)
    u = jnp.sqrt(-jnp.tanh(log_a) * (a * a + 1.0)) * (i * xc)

    for d in (1, 2, 4):
        m = tpos >= d
        u = jnp.where(m, u + a * pltpu.roll(u, d, axis=0), u)
        a = jnp.where(m, a * pltpu.roll(a, d, axis=0), a)

    gate = _gelu_tanh(gr_ref[...])
    if carry:
        a_s[...] = a
        u_s[...] = u

        def body(g, hc):
            rows = pl.ds(pl.multiple_of(g * SUBLANES, SUBLANES), SUBLANES)
            h = u_s[rows, :] + a_s[rows, :] * hc
            u_s[rows, :] = h
            return jnp.broadcast_to(h[SUBLANES - 1:SUBLANES, :], h.shape)

        hc = lax.fori_loop(0, n // SUBLANES, body, hc_ref[...])
        hc_ref[...] = hc
        y_ref[...] = (u_s[...] * gate).astype(y_ref.dtype)
        hl_ref[...] = hc[:1]
    else:
        h = u + a * h0_ref[...]
        hs_ref[...] = h
        y_ref[...] = (h * gate).astype(y_ref.dtype)


def _rg_weight_specs(imap):
    row = pl.BlockSpec((1, D_RNN), imap(0, 0))
    blk = pl.BlockSpec((N_RNN_BLOCKS, RNN_BLOCK, RNN_BLOCK), imap(0, 0, 0))
    return [pl.BlockSpec((CONV_W, D_RNN), imap(0, 0)), row, blk, row, blk, row, row]


def _rglru_prompt(z, rgw):
    nt = SEQ // RG_TC
    const = lambda *idx: (lambda b, t: idx)
    return pl.pallas_call(
        functools.partial(_rglru_kernel, carry=True),
        grid=(BATCH, nt),
        in_specs=[pl.BlockSpec((RG_TC, D_RNN), lambda b, t: (b * nt + t, COL_XR)),
                  pl.BlockSpec((RG_TC, D_RNN), lambda b, t: (b * nt + t, COL_GR))] + _rg_weight_specs(const),
        out_specs=[pl.BlockSpec((RG_TC, D_RNN), lambda b, t: (b * nt + t, 0)),
                   pl.BlockSpec((None, 1, D_RNN), lambda b, t: (b, 0, 0))],
        out_shape=[jax.ShapeDtypeStruct((TP, D_RNN), BF16), jax.ShapeDtypeStruct((BATCH, 1, D_RNN), F32)],
        scratch_shapes=[pltpu.VMEM((SUBLANES, D_RNN), F32), pltpu.VMEM((SUBLANES, D_RNN), F32),
                        pltpu.VMEM((RG_TC, D_RNN), F32), pltpu.VMEM((RG_TC, D_RNN), F32)],
        compiler_params=_cparams(("arbitrary", "arbitrary")),
        name="rglru_prompt",
    )(z, z, *rgw)


def _rglru_sample(z, hist, h0, rgw):
    nt = TS // RG_TC
    row0 = TP // RG_TC
    const = lambda *idx: (lambda t: idx)
    return pl.pallas_call(
        functools.partial(_rglru_kernel, carry=False),
        grid=(nt,),
        in_specs=[pl.BlockSpec((RG_TC, D_RNN), lambda t: (row0 + t, COL_XR)),
                  pl.BlockSpec((RG_TC, D_RNN), lambda t: (row0 + t, COL_GR)),
                  pl.BlockSpec((RG_TC, D_RNN), lambda t: (t, 0)),
                  pl.BlockSpec((RG_TC, D_RNN), lambda t: (t, 0))] + _rg_weight_specs(const),
        out_specs=[pl.BlockSpec((RG_TC, D_RNN), lambda t: (t, 0)),
                   pl.BlockSpec((RG_TC, D_RNN), lambda t: (t, 0))],
        out_shape=[jax.ShapeDtypeStruct((TS, D_RNN), BF16), jax.ShapeDtypeStruct((TS, D_RNN), F32)],
        compiler_params=_cparams(("arbitrary",)),
        name="rglru_sample",
    )(z, z, hist, h0, *rgw)


MIX_TM = 1024
MIX_TN = 512


def _mix_kernel(oa_ref, yr_ref, wa_ref, wr_ref, ga_ref, gb_ref, o_ref):
    ua = _dot(oa_ref[...], wa_ref[...])
    ur = _dot(yr_ref[...], wr_ref[...])
    o_ref[...] = (jax.nn.sigmoid(ga_ref[...]) * ua + jax.nn.sigmoid(gb_ref[...]) * ur).astype(o_ref.dtype)


def _mix(o_attn, y_rnn, w_attn16, w_rnn16, z, layer):
    nj = D_MODEL // MIX_TN
    return pl.pallas_call(
        _mix_kernel,
        grid=(nj, T // MIX_TM),
        in_specs=[
            pl.BlockSpec((MIX_TM, D_MODEL), lambda j, i: (i, 0)),
            pl.BlockSpec((MIX_TM, D_MODEL), lambda j, i: (i, 0)),
            pl.BlockSpec((None, D_MODEL, MIX_TN), lambda j, i: (layer, 0, j)),
            pl.BlockSpec((None, D_MODEL, MIX_TN), lambda j, i: (layer, 0, j)),
            pl.BlockSpec((MIX_TM, MIX_TN), lambda j, i: (i, COL_GA * nj + j)),
            pl.BlockSpec((MIX_TM, MIX_TN), lambda j, i: (i, COL_GB * nj + j)),
        ],
        out_specs=pl.BlockSpec((MIX_TM, MIX_TN), lambda j, i: (i, j)),
        out_shape=jax.ShapeDtypeStruct((T, D_MODEL), BF16),
        compiler_params=_cparams(("arbitrary", "arbitrary")),
        name="mix",
    )(o_attn, y_rnn, w_attn16, w_rnn16, z, z)


OUT_TM = 512


def _out_ln_kernel(m_ref, w_ref, x_ref, g_ref, b_ref, o32_ref, o16_ref):
    y = _layer_norm(ALPHA * x_ref[...] + _dot(m_ref[...], w_ref[...]), g_ref[...], b_ref[...])
    o32_ref[...] = y
    o16_ref[...] = y.astype(BF16)


def _out_ln(mixed, w_out16, x32, g, b, layer):
    row = pl.BlockSpec((None, 1, D_MODEL), lambda i: (layer, 0, 0))
    return pl.pallas_call(
        _out_ln_kernel,
        grid=(T // OUT_TM,),
        in_specs=[
            pl.BlockSpec((OUT_TM, D_MODEL), lambda i: (i, 0)),
            pl.BlockSpec((None, D_MODEL, D_MODEL), lambda i: (layer, 0, 0)),
            pl.BlockSpec((OUT_TM, D_MODEL), lambda i: (i, 0)),
            row, row,
        ],
        out_specs=[pl.BlockSpec((OUT_TM, D_MODEL), lambda i: (i, 0)),
                   pl.BlockSpec((OUT_TM, D_MODEL), lambda i: (i, 0))],
        out_shape=[jax.ShapeDtypeStruct((T, D_MODEL), F32), jax.ShapeDtypeStruct((T, D_MODEL), BF16)],
        compiler_params=_cparams(("arbitrary",)),
        name="out_ln",
    )(mixed, w_out16, x32, g, b)


PLE_TM = 512
PLE_TN = 1024


def _ple_kernel(x_ref, pp_ref, ps_ref, wg_ref, wp_ref, o_ref):
    i = pl.program_id(1)
    gate = jax.nn.sigmoid(_dot(x_ref[...], wg_ref[...]))

    @pl.when(i < TP // PLE_TM)
    def _():
        o_ref[...] = gate * _dot(pp_ref[...].astype(BF16), wp_ref[...])

    @pl.when(i >= TP // PLE_TM)
    def _():
        o_ref[...] = gate * _dot(ps_ref[...].astype(BF16), wp_ref[...])


def _ple(x16, pp, ps, w_gate16, w_proj16, layer):
    n_p = TP // PLE_TM
    return pl.pallas_call(
        _ple_kernel,
        grid=(D_MODEL // PLE_TN, T // PLE_TM),
        in_specs=[
            pl.BlockSpec((PLE_TM, D_MODEL), lambda j, i: (i, 0)),
            pl.BlockSpec((None, PLE_TM, PLE_DIM), lambda j, i: (layer, jnp.minimum(i, n_p - 1), 0)),
            pl.BlockSpec((None, PLE_TM, PLE_DIM), lambda j, i: (layer, jnp.maximum(i - n_p, 0), 0)),
            pl.BlockSpec((None, D_MODEL, PLE_TN), lambda j, i: (layer, 0, j)),
            pl.BlockSpec((None, PLE_DIM, PLE_TN), lambda j, i: (layer, 0, j)),
        ],
        out_specs=pl.BlockSpec((PLE_TM, PLE_TN), lambda j, i: (i, j)),
        out_shape=jax.ShapeDtypeStruct((T, D_MODEL), F32),
        compiler_params=_cparams(("arbitrary", "arbitrary")),
        name="ple",
    )(x16, pp, ps, w_gate16, w_proj16)


RT_TM = 512


def _router_kernel(x_ref, w_ref, b_ref, comb_ref):
    logits = lax.dot_general(w_ref[...], x_ref[...], (((1,), (1,)), ((), ())),
                             precision=lax.Precision.HIGHEST, preferred_element_type=F32) + b_ref[...]
    mx = jnp.max(logits, axis=0, keepdims=True)
    e = jnp.exp(logits - mx)
    probs = e / jnp.sum(e, axis=0, keepdims=True)
    rows = [probs[j:j + 1, :] for j in range(N_EXPERTS)]

    best_score = None
    for g in range(N_GROUPS):
        v = rows[g * EXPERTS_PER_GROUP:(g + 1) * EXPERTS_PER_GROUP]
        m1 = functools.reduce(jnp.maximum, v)
        i1 = jnp.full(m1.shape, EXPERTS_PER_GROUP - 1, jnp.int32)
        for j in reversed(range(EXPERTS_PER_GROUP)):
            i1 = jnp.where(v[j] == m1, j, i1)
        rest = [jnp.where(i1 == j, -1.0, v[j]) for j in range(EXPERTS_PER_GROUP)]
        m2 = functools.reduce(jnp.maximum, rest)
        i2 = jnp.full(m1.shape, EXPERTS_PER_GROUP - 1, jnp.int32)
        for j in reversed(range(EXPERTS_PER_GROUP)):
            i2 = jnp.where(rest[j] == m2, j, i2)
        score = m1 + m2
        if best_score is None:
            best_score, b_m1, b_m2 = score, m1, m2
            b_e1, b_e2 = i1, i2 + 0
        else:
            better = score > best_score
            best_score = jnp.where(better, score, best_score)
            b_m1 = jnp.where(better, m1, b_m1)
            b_m2 = jnp.where(better, m2, b_m2)
            b_e1 = jnp.where(better, g * EXPERTS_PER_GROUP + i1, b_e1)
            b_e2 = jnp.where(better, g * EXPERTS_PER_GROUP + i2, b_e2)
    denom = b_m1 + b_m2
    w1 = b_m1 / denom
    w2 = b_m2 / denom
    eid = lax.broadcasted_iota(jnp.int32, (LANES, RT_TM), 0)
    comb_t = jnp.where(eid == b_e1, w1, 0.0) + jnp.where(eid == b_e2, w2, 0.0)
    comb_ref[...] = comb_t.T


def _router(x32, router_w_t, router_b_col):
    return pl.pallas_call(
        _router_kernel,
        grid=(T // RT_TM,),
        in_specs=[
            pl.BlockSpec((RT_TM, D_MODEL), lambda i: (i, 0)),
            pl.BlockSpec((N_EXPERTS, D_MODEL), lambda i: (0, 0)),
            pl.BlockSpec((N_EXPERTS, 1), lambda i: (0, 0)),
        ],
        out_specs=pl.BlockSpec((RT_TM, LANES), lambda i: (i, 0)),
        out_shape=jax.ShapeDtypeStruct((T, LANES), F32),
        compiler_params=_cparams(("arbitrary",)),
        name="router",
    )(x32, router_w_t, router_b_col)


MOE_TM = 512
MOE_TF = 512
MOE_NF = D_EXPERT // MOE_TF


def _moe_kernel(x16_ref, comb_ref, wg_ref, wu_ref, wd_ref, x32_ref, ple_ref, g_ref, b_ref,
                o32_ref, o16_ref, acc_ref):
    e = pl.program_id(1)
    f = pl.program_id(2)

    @pl.when((e == 0) & (f == 0))
    def _():
        acc_ref[...] = ALPHA * x32_ref[...] + ple_ref[...]

    x = x16_ref[...]
    hg = _dot(x, wg_ref[...])
    hu = _dot(x, wu_ref[...])
    lane = lax.broadcasted_iota(jnp.int32, comb_ref.shape, 1)
    cw = jnp.sum(jnp.where(lane == e, comb_ref[...], 0.0), axis=-1, keepdims=True)
    he = (jax.nn.silu(hg) * hu).astype(BF16)
    acc_ref[...] += cw * _dot(he, wd_ref[...])

    @pl.when((e == N_EXPERTS - 1) & (f == MOE_NF - 1))
    def _():
        y = _layer_norm(acc_ref[...], g_ref[...], b_ref[...])
        o32_ref[...] = y
        o16_ref[...] = y.astype(BF16)


def _moe(x16, comb, wg16, wu16, wd16, x32, ple, g, b, layer):
    row = pl.BlockSpec((None, 1, D_MODEL), lambda i, e, f: (layer, 0, 0))
    tile = pl.BlockSpec((MOE_TM, D_MODEL), lambda i, e, f: (i, 0))
    return pl.pallas_call(
        _moe_kernel,
        grid=(T // MOE_TM, N_EXPERTS, MOE_NF),
        in_specs=[
            tile,
            pl.BlockSpec((MOE_TM, LANES), lambda i, e, f: (i, 0)),
            pl.BlockSpec((None, None, D_MODEL, MOE_TF), lambda i, e, f: (layer, e, 0, f)),
            pl.BlockSpec((None, None, D_MODEL, MOE_TF), lambda i, e, f: (layer, e, 0, f)),
            pl.BlockSpec((None, None, MOE_TF, D_MODEL), lambda i, e, f: (layer, e, f, 0)),
            tile, tile, row, row,
        ],
        out_specs=[tile, tile],
        out_shape=[jax.ShapeDtypeStruct((T, D_MODEL), F32), jax.ShapeDtypeStruct((T, D_MODEL), BF16)],
        scratch_shapes=[pltpu.VMEM((MOE_TM, D_MODEL), F32)],
        compiler_params=_cparams(("arbitrary", "arbitrary", "arbitrary")),
        name="moe",
    )(x16, comb, wg16, wu16, wd16, x32, ple, g, b)


def kernel(x_prompt, x_sample, cache_k, cache_v, state_conv, state_h, page_table, p_prompt, p_sample,
           emb_ln_g, emb_ln_b, w_in, lambda_q1, lambda_k1, lambda_q2, lambda_k2, subln_w,
           conv_w, conv_b, rg_w_a, rg_b_a, rg_w_x, rg_b_x, rg_lambda,
           w_branch_attn, w_branch_rnn, w_out, ln1_g, ln1_b, router_w, router_b,
           moe_w_gate, moe_w_up, moe_w_down, w_ple_gate, w_ple_proj, ln2_g, ln2_b):
    row = lambda a: a.reshape(1, -1)
    rows3 = lambda a: a.reshape(DEPTH, 1, -1)

    w_in16 = w_in.astype(BF16)
    w_attn16 = w_branch_attn.astype(BF16)
    w_rnn16 = w_branch_rnn.astype(BF16)
    w_out16 = w_out.astype(BF16)
    w_pg16 = w_ple_gate.astype(BF16)
    w_pp16 = w_ple_proj.astype(BF16)
    wg16 = moe_w_gate.astype(BF16)
    wu16 = moe_w_up.astype(BF16)
    wd16 = moe_w_down.astype(BF16)
    rg_wa16 = rg_w_a.astype(BF16)
    rg_wx16 = rg_w_x.astype(BF16)
    router_w_t = router_w.T
    router_b_col = router_b.reshape(N_EXPERTS, 1)
    ln1_g3, ln1_b3, ln2_g3, ln2_b3 = rows3(ln1_g), rows3(ln1_b), rows3(ln2_g), rows3(ln2_b)
    pp = p_prompt.reshape(DEPTH, TP, PLE_DIM)
    ps = p_sample.reshape(DEPTH, TS, PLE_DIM)

    x32, x16 = _embed_ln(x_prompt.reshape(TP, D_MODEL), x_sample.reshape(TS, D_MODEL), row(emb_ln_g), row(emb_ln_b))

    outs = {k: [] for k in ("kp", "vp", "cp", "hp", "ks", "vs", "cs", "hs")}
    for l in range(DEPTH):
        lam_init = 0.8 - 0.6 * math.exp(-0.3 * l)
        lam_vecs = (row(lambda_q1[l]), row(lambda_k1[l]), row(lambda_q2[l]), row(lambda_k2[l]))
        subln = row(subln_w[l])
        rgw = (conv_w[l], row(conv_b[l]), rg_wa16[l], row(rg_b_a[l]), rg_wx16[l], row(rg_b_x[l]), row(rg_lambda[l]))

        z = _in_proj(x16, w_in16, l)

        o_p = _attn_prompt(z, lam_vecs, subln, lam_init)
        o_s = _attn_sample(z, cache_k, cache_v, page_table, lam_vecs, subln, lam_init, l)
        o_attn = jnp.concatenate([o_p, o_s.astype(BF16)], axis=0)

        y_p, h_p = _rglru_prompt(z, rgw)
        hist = jnp.pad(state_conv[l], ((0, 0), (SUBLANES - (CONV_W - 1), 0), (0, 0))).reshape(TS, D_RNN)
        h0 = jnp.repeat(state_h[l], DEC_SEQ, axis=0)
        y_s, hs_s = _rglru_sample(z, hist, h0, rgw)
        y_rnn = jnp.concatenate([y_p, y_s], axis=0)

        mixed = _mix(o_attn, y_rnn, w_attn16, w_rnn16, z, l)
        x32, x16 = _out_ln(mixed, w_out16, x32, ln1_g3, ln1_b3, l)
        ple = _ple(x16, pp, ps, w_pg16, w_pp16, l)
        comb = _router(x32, router_w_t, router_b_col)
        x32, x16 = _moe(x16, comb, wg16, wu16, wd16, x32, ple, ln2_g3, ln2_b3, l)

        kv = z[:, COL_K * D_MODEL:(COL_V + 1) * D_MODEL]
        xr = z[:, COL_XR * D_MODEL:(COL_XR + 1) * D_MODEL]
        outs["kp"].append(kv[:TP, :D_MODEL].reshape(BATCH, SEQ, N_HEADS, HEAD_W))
        outs["vp"].append(kv[:TP, D_MODEL:].reshape(BATCH, SEQ, N_HEADS, HEAD_W))
        outs["ks"].append(kv[TP:, :D_MODEL].reshape(DEC_BATCH, DEC_SEQ, N_HEADS, HEAD_W))
        outs["vs"].append(kv[TP:, D_MODEL:].reshape(DEC_BATCH, DEC_SEQ, N_HEADS, HEAD_W))
        outs["cp"].append(xr[:TP].reshape(BATCH, SEQ, D_RNN)[:, SEQ - (CONV_W - 1):])
        outs["cs"].append(xr[TP:].reshape(DEC_BATCH, DEC_SEQ, D_RNN)[:, DEC_SEQ - (CONV_W - 1):])
        outs["hp"].append(h_p.reshape(BATCH, D_RNN))
        outs["hs"].append(hs_s.reshape(DEC_BATCH, DEC_SEQ, D_RNN)[:, DEC_SEQ - 1])

    st = {k: jnp.stack(v) for k, v in outs.items()}
    return (x32[:TP].reshape(BATCH, SEQ, D_MODEL), x32[TP:].reshape(DEC_BATCH, DEC_SEQ, D_MODEL),
            st["kp"], st["vp"], st["cp"], st["hp"], st["ks"], st["vs"], st["cs"], st["hs"])
```

```python
import functools
import math

import jax
import jax.numpy as jnp
from jax import lax
from jax.experimental import pallas as pl
from jax.experimental.pallas import tpu as pltpu

F32 = jnp.float32
BF16 = jnp.bfloat16

D_MODEL = 2048
BATCH = 4
SEQ = 2048
DEPTH = 2
DEC_BATCH = 128
DEC_SEQ = 8
PAST_LEN = 2048
PAGE_SIZE = 128
N_PAGES = PAST_LEN // PAGE_SIZE
HEAD_DIM = 128
N_HEADS = D_MODEL // (2 * HEAD_DIM)
HEAD_W = 2 * HEAD_DIM
D_RNN = D_MODEL
N_RNN_BLOCKS = 8
RNN_BLOCK = D_RNN // N_RNN_BLOCKS
CONV_W = 4
RG_C = 8.0
N_EXPERTS = 16
N_GROUPS = 4
EXPERTS_PER_GROUP = N_EXPERTS // N_GROUPS
D_EXPERT = 1024
PLE_DIM = 256
LN_EPS = 1e-5
N_IN = 7 * D_MODEL
ALPHA = (2.0 * DEPTH) ** 0.25
ATTN_SCALE = HEAD_DIM ** -0.5

TP = BATCH * SEQ
TS = DEC_BATCH * DEC_SEQ
T = TP + TS

COL_Q, COL_K, COL_V, COL_XR, COL_GR, COL_GA, COL_GB = range(7)

SUBLANES = 8
LANES = 128
VMEM_LIMIT = 56 * 1024 * 1024
NEG_BIG = -1e30
LOG2E = math.log2(math.e)


def _cparams(sem):
    return pltpu.CompilerParams(dimension_semantics=sem, vmem_limit_bytes=VMEM_LIMIT)


def _layer_norm(x, g, b):
    mu = jnp.mean(x, axis=-1, keepdims=True)
    xc = x - mu
    var = jnp.mean(xc * xc, axis=-1, keepdims=True)
    return xc * lax.rsqrt(var + LN_EPS) * g + b


def _dot(a, b):
    return jnp.dot(a, b, preferred_element_type=F32)


def _dot_nt(a, b):
    return lax.dot_general(a, b, (((1,), (1,)), ((), ())), preferred_element_type=F32)


def _diff_lambda(lq1, lk1, lq2, lk2, lam_init):
    return (jnp.exp(jnp.sum(lq1 * lk1, axis=-1, keepdims=True))
            - jnp.exp(jnp.sum(lq2 * lk2, axis=-1, keepdims=True)) + lam_init)


EMB_TM = 512


def _embed_ln_kernel(xp_ref, xs_ref, g_ref, b_ref, o32_ref, o16_ref):
    i = pl.program_id(0)

    def go(x):
        y = _layer_norm(x, g_ref[...], b_ref[...])
        o32_ref[...] = y
        o16_ref[...] = y.astype(BF16)

    @pl.when(i < TP // EMB_TM)
    def _():
        go(xp_ref[...])

    @pl.when(i >= TP // EMB_TM)
    def _():
        go(xs_ref[...])


def _embed_ln(xp, xs, g, b):
    n_p = TP // EMB_TM
    return pl.pallas_call(
        _embed_ln_kernel,
        grid=(T // EMB_TM,),
        in_specs=[
            pl.BlockSpec((EMB_TM, D_MODEL), lambda i: (jnp.minimum(i, n_p - 1), 0)),
            pl.BlockSpec((EMB_TM, D_MODEL), lambda i: (jnp.maximum(i - n_p, 0), 0)),
            pl.BlockSpec((1, D_MODEL), lambda i: (0, 0)),
            pl.BlockSpec((1, D_MODEL), lambda i: (0, 0)),
        ],
        out_specs=[pl.BlockSpec((EMB_TM, D_MODEL), lambda i: (i, 0)),
                   pl.BlockSpec((EMB_TM, D_MODEL), lambda i: (i, 0))],
        out_shape=[jax.ShapeDtypeStruct((T, D_MODEL), F32), jax.ShapeDtypeStruct((T, D_MODEL), BF16)],
        compiler_params=_cparams(("arbitrary",)),
        name="embed_ln",
    )(xp, xs, g, b)


IN_TM = 1024
IN_TN = 1024


def _matmul_kernel(a_ref, w_ref, o_ref):
    o_ref[...] = _dot(a_ref[...], w_ref[...]).astype(o_ref.dtype)


def _in_proj(x16, w_in16, layer):
    return pl.pallas_call(
        _matmul_kernel,
        grid=(N_IN // IN_TN, T // IN_TM),
        in_specs=[
            pl.BlockSpec((IN_TM, D_MODEL), lambda j, i: (i, 0)),
            pl.BlockSpec((None, D_MODEL, IN_TN), lambda j, i: (layer, 0, j)),
        ],
        out_specs=pl.BlockSpec((IN_TM, IN_TN), lambda j, i: (i, j)),
        out_shape=jax.ShapeDtypeStruct((T, N_IN), F32),
        compiler_params=_cparams(("arbitrary", "arbitrary")),
        name="in_proj",
    )(x16, w_in16)


ATT_TQ = 512
ATT_TK = 512
ATT_NQ = SEQ // ATT_TQ


def _attn_tables():
    qi, kj = [], []
    for q in range(ATT_NQ):
        for k in range(q + 1):
            qi.append(q)
            kj.append(k)
    return jnp.asarray(qi, jnp.int32), jnp.asarray(kj, jnp.int32)


def _headnorm_store(o, subln, lam_init):
    ms = jnp.mean(o * o, axis=-1, keepdims=True)
    return o * lax.rsqrt(ms + LN_EPS) * subln * (1.0 - lam_init)


def _attn_prompt_kernel(qi_tab, kj_tab, q_ref, k_ref, v_ref, lq1, lk1, lq2, lk2, subln_ref, o_ref,
                        qs_ref, m_ref, l_ref, acc_ref, *, lam_init):
    s_idx = pl.program_id(2)
    qi = qi_tab[s_idx]
    kj = kj_tab[s_idx]

    @pl.when(kj == 0)
    def _():
        m_ref[...] = jnp.full(m_ref.shape, NEG_BIG, F32)
        l_ref[...] = jnp.zeros(l_ref.shape, F32)
        acc_ref[...] = jnp.zeros(acc_ref.shape, F32)
        qs_ref[...] = (q_ref[...] * (ATTN_SCALE * LOG2E)).astype(BF16)

    def step(diagonal):
        k = k_ref[...].astype(BF16)
        v = v_ref[...].astype(BF16)
        if diagonal:
            causal = (lax.broadcasted_iota(jnp.int32, (ATT_TQ, ATT_TK), 1)
                      <= lax.broadcasted_iota(jnp.int32, (ATT_TQ, ATT_TK), 0))
        for c in range(2):
            sl = slice(c * HEAD_DIM, (c + 1) * HEAD_DIM)
            s = _dot_nt(qs_ref[:, sl], k[:, sl])
            if diagonal:
                s = jnp.where(causal, s, NEG_BIG)
            m_prev = m_ref[c]
            m_new = jnp.maximum(m_prev, jnp.max(s, axis=-1, keepdims=True))
            corr = jnp.exp2(m_prev - m_new)
            p = jnp.exp2(s - m_new)
            l_ref[c] = l_ref[c] * corr + jnp.sum(p, axis=-1, keepdims=True)
            acc_ref[c] = acc_ref[c] * corr + _dot(p.astype(BF16), v)
            m_ref[c] = m_new

    @pl.when(kj < qi)
    def _():
        step(False)

    @pl.when(kj == qi)
    def _():
        step(True)
        lam = _diff_lambda(lq1[...], lk1[...], lq2[...], lk2[...], lam_init)
        o = acc_ref[0] / l_ref[0] - lam * (acc_ref[1] / l_ref[1])
        o_ref[...] = _headnorm_store(o, subln_ref[...], lam_init).astype(o_ref.dtype)


def _attn_prompt(z, lam_vecs, subln, lam_init):
    qi_tab, kj_tab = _attn_tables()
    n_steps = qi_tab.shape[0]
    vec = pl.BlockSpec((1, HEAD_DIM), lambda b, h, s, qt, kt: (0, 0))
    grid_spec = pltpu.PrefetchScalarGridSpec(
        num_scalar_prefetch=2,
        grid=(BATCH, N_HEADS, n_steps),
        in_specs=[
            pl.BlockSpec((ATT_TQ, HEAD_W), lambda b, h, s, qt, kt: (b * ATT_NQ + qt[s], COL_Q * N_HEADS + h)),
            pl.BlockSpec((ATT_TK, HEAD_W), lambda b, h, s, qt, kt: (b * ATT_NQ + kt[s], COL_K * N_HEADS + h)),
            pl.BlockSpec((ATT_TK, HEAD_W), lambda b, h, s, qt, kt: (b * ATT_NQ + kt[s], COL_V * N_HEADS + h)),
            vec, vec, vec, vec,
            pl.BlockSpec((1, HEAD_W), lambda b, h, s, qt, kt: (0, 0)),
        ],
        out_specs=pl.BlockSpec((ATT_TQ, HEAD_W), lambda b, h, s, qt, kt: (b * ATT_NQ + qt[s], h)),
        scratch_shapes=[pltpu.VMEM((ATT_TQ, HEAD_W), BF16),
                        pltpu.VMEM((2, ATT_TQ, 1), F32), pltpu.VMEM((2, ATT_TQ, 1), F32),
                        pltpu.VMEM((2, ATT_TQ, HEAD_W), F32)],
    )
    return pl.pallas_call(
        functools.partial(_attn_prompt_kernel, lam_init=lam_init),
        grid_spec=grid_spec,
        out_shape=jax.ShapeDtypeStruct((TP, D_MODEL), BF16),
        compiler_params=_cparams(("arbitrary", "arbitrary", "arbitrary")),
        name="attn_prompt",
    )(qi_tab, kj_tab, z, z, z, *lam_vecs, subln)


QROWS = N_HEADS * 2 * DEC_SEQ
HROWS = 2 * DEC_SEQ
KROWS = PAGE_SIZE * N_HEADS
PAGES_PER_STEP = 4
N_PSTEPS = N_PAGES // PAGES_PER_STEP


def _attn_sample_kernel(pt_ref, q_ref, *refs, lam_init):
    kc_refs = refs[:PAGES_PER_STEP]
    vc_refs = refs[PAGES_PER_STEP:2 * PAGES_PER_STEP]
    (kn_ref, vn_ref, lq1, lk1, lq2, lk2, subln_ref, o_ref,
     qb_ref, bias_ref, m_ref, l_ref, acc_ref) = refs[2 * PAGES_PER_STEP:]
    b_idx = pl.program_id(0)
    p_idx = pl.program_id(1)

    @pl.when((b_idx == 0) & (p_idx == 0))
    def _():
        row_head = lax.broadcasted_iota(jnp.int32, (QROWS, KROWS), 0) // HROWS
        col_head = lax.broadcasted_iota(jnp.int32, (QROWS, KROWS), 1) % N_HEADS
        bias_ref[...] = jnp.where(row_head == col_head, 0.0, NEG_BIG)

    @pl.when(p_idx == 0)
    def _():
        m_ref[...] = jnp.full(m_ref.shape, NEG_BIG, F32)
        l_ref[...] = jnp.zeros(l_ref.shape, F32)
        acc_ref[...] = jnp.zeros(acc_ref.shape, F32)
        q = q_ref[...]
        lane = lax.broadcasted_iota(jnp.int32, (DEC_SEQ, HEAD_W), 1)
        for h in range(N_HEADS):
            qh = q[:, h * HEAD_W:(h + 1) * HEAD_W]
            qb = jnp.concatenate([jnp.where(lane < HEAD_DIM, qh, 0.0), jnp.where(lane >= HEAD_DIM, qh, 0.0)], axis=0)
            qb_ref[h * HROWS:(h + 1) * HROWS, :] = qb.astype(BF16)

    def softmax_update(s, pv_fn):
        m_prev = m_ref[...]
        m_new = jnp.maximum(m_prev, jnp.max(s, axis=-1, keepdims=True))
        corr = jnp.exp(m_prev - m_new)
        p = jnp.exp(s - m_new)
        l_ref[...] = l_ref[...] * corr + jnp.sum(p, axis=-1, keepdims=True)
        acc_ref[...] = acc_ref[...] * corr + pv_fn(p.astype(BF16))
        m_ref[...] = m_new

    qb = qb_ref[...]
    bias = bias_ref[...]
    s = jnp.concatenate([_dot_nt(qb, kc[...].astype(BF16)) * ATTN_SCALE + bias for kc in kc_refs], axis=1)

    def pv_pages(pb):
        out = _dot(pb[:, :KROWS], vc_refs[0][...].astype(BF16))
        for j in range(1, PAGES_PER_STEP):
            out = out + _dot(pb[:, j * KROWS:(j + 1) * KROWS], vc_refs[j][...].astype(BF16))
        return out

    softmax_update(s, pv_pages)

    @pl.when(p_idx == N_PSTEPS - 1)
    def _():
        pad = jnp.zeros((PAGE_SIZE - DEC_SEQ, D_MODEL), F32)
        kb = jnp.concatenate([kn_ref[...], pad], axis=0).astype(BF16)
        vb = jnp.concatenate([vn_ref[...], pad], axis=0).astype(BF16)
        qpos = lax.broadcasted_iota(jnp.int32, (QROWS, PAGE_SIZE), 0) % DEC_SEQ
        kpos = lax.broadcasted_iota(jnp.int32, (QROWS, PAGE_SIZE), 1)
        sn = jnp.concatenate(
            [_dot_nt(qb_ref[h * HROWS:(h + 1) * HROWS, :], kb[:, h * HEAD_W:(h + 1) * HEAD_W])
             for h in range(N_HEADS)], axis=0) * ATTN_SCALE
        sn = jnp.where(kpos <= qpos, sn, NEG_BIG)

        def pv_new(pb):
            return jnp.concatenate(
                [_dot(pb[h * HROWS:(h + 1) * HROWS, :], vb[:, h * HEAD_W:(h + 1) * HEAD_W])
                 for h in range(N_HEADS)], axis=0)

        softmax_update(sn, pv_new)
        lam = _diff_lambda(lq1[...], lk1[...], lq2[...], lk2[...], lam_init)
        o = acc_ref[...] / l_ref[...]
        for h in range(N_HEADS):
            oh = o[h * HROWS:h * HROWS + DEC_SEQ, :] - lam * o[h * HROWS + DEC_SEQ:(h + 1) * HROWS, :]
            o_ref[:, h * HEAD_W:(h + 1) * HEAD_W] = _headnorm_store(oh, subln_ref[...], lam_init)


def _attn_sample(z, cache_k, cache_v, page_table, lam_vecs, subln, lam_init, layer):
    n_pool = cache_k.shape[1]
    ck = cache_k.reshape(DEPTH, n_pool, KROWS, HEAD_W)
    cv = cache_v.reshape(DEPTH, n_pool, KROWS, HEAD_W)
    pt = page_table.reshape(-1)
    row0 = TP // DEC_SEQ
    vec = pl.BlockSpec((1, HEAD_DIM), lambda b, p, pt: (0, 0))

    def page_spec(j):
        return pl.BlockSpec((None, None, KROWS, HEAD_W),
                            lambda b, p, pt: (layer, pt[b * N_PAGES + p * PAGES_PER_STEP + j], 0, 0))

    pages = [page_spec(j) for j in range(PAGES_PER_STEP)]
    grid_spec = pltpu.PrefetchScalarGridSpec(
        num_scalar_prefetch=1,
        grid=(DEC_BATCH, N_PSTEPS),
        in_specs=[pl.BlockSpec((DEC_SEQ, D_MODEL), lambda b, p, pt: (row0 + b, COL_Q))] + pages + pages + [
            pl.BlockSpec((DEC_SEQ, D_MODEL), lambda b, p, pt: (row0 + b, COL_K)),
            pl.BlockSpec((DEC_SEQ, D_MODEL), lambda b, p, pt: (row0 + b, COL_V)),
            vec, vec, vec, vec,
            pl.BlockSpec((1, HEAD_W), lambda b, p, pt: (0, 0)),
        ],
        out_specs=pl.BlockSpec((DEC_SEQ, D_MODEL), lambda b, p, pt: (b, 0)),
        scratch_shapes=[pltpu.VMEM((QROWS, HEAD_W), BF16), pltpu.VMEM((QROWS, KROWS), F32),
                        pltpu.VMEM((QROWS, 1), F32), pltpu.VMEM((QROWS, 1), F32), pltpu.VMEM((QROWS, HEAD_W), F32)],
    )
    return pl.pallas_call(
        functools.partial(_attn_sample_kernel, lam_init=lam_init),
        grid_spec=grid_spec,
        out_shape=jax.ShapeDtypeStruct((TS, D_MODEL), F32),
        compiler_params=_cparams(("arbitrary", "arbitrary")),
        name="attn_sample",
    )(pt, z, *([ck] * PAGES_PER_STEP), *([cv] * PAGES_PER_STEP), z, z, *lam_vecs, subln)


RG_TC = 256


def _gelu_tanh(x):
    return 0.5 * x * (1.0 + jnp.tanh(math.sqrt(2.0 / math.pi) * (x + 0.044715 * (x * x * x))))


def _rglru_kernel(*refs, carry):
    if carry:
        (xr_ref, gr_ref, cw_ref, cb_ref, wa_ref, ba_ref, wx_ref, bx_ref, lam_ref,
         y_ref, hl_ref, xprev_ref, hc_ref, a_s, u_s) = refs
    else:
        (xr_ref, gr_ref, hist_ref, h0_ref, cw_ref, cb_ref, wa_ref, ba_ref, wx_ref, bx_ref, lam_ref,
         y_ref, hs_ref) = refs
    t = pl.program_id(1) if carry else None
    x = xr_ref[...]
    n = x.shape[0]
    tpos = lax.broadcasted_iota(jnp.int32, x.shape, 0) % SUBLANES

    if carry:
        @pl.when(t == 0)
        def _():
            xprev_ref[...] = jnp.zeros(xprev_ref.shape, F32)
            hc_ref[...] = jnp.zeros(hc_ref.shape, F32)

    xc = cb_ref[...] + cw_ref[CONV_W - 1:CONV_W, :] * x
    for s in range(1, CONV_W):
        w_s = cw_ref[CONV_W - 1 - s:CONV_W - s, :]
        if carry:
            xc = xc + w_s * pltpu.roll(x, s, axis=0)
        else:
            hist = pltpu.roll(hist_ref[...], n - SUBLANES + s, axis=0)
            xc = xc + w_s * jnp.where(tpos >= s, pltpu.roll(x, s, axis=0), hist)
    if carry:
        x0 = x[:SUBLANES]
        xp = xprev_ref[...]
        t0 = lax.broadcasted_iota(jnp.int32, x0.shape, 0)
        xc0 = cb_ref[...] + cw_ref[CONV_W - 1:CONV_W, :] * x0
        for s in range(1, CONV_W):
            w_s = cw_ref[CONV_W - 1 - s:CONV_W - s, :]
            xc0 = xc0 + w_s * jnp.where(t0 >= s, pltpu.roll(x0, s, axis=0), pltpu.roll(xp, s, axis=0))
        xc = jnp.concatenate([xc0, xc[SUBLANES:]], axis=0)
        xprev_ref[...] = x[n - SUBLANES:]

    xcb = xc.astype(BF16)
    ga = jnp.concatenate([_dot(xcb[:, j * RNN_BLOCK:(j + 1) * RNN_BLOCK], wa_ref[j]) for j in range(N_RNN_BLOCKS)],
                         axis=1)
    gx = jnp.concatenate([_dot(xcb[:, j * RNN_BLOCK:(j + 1) * RNN_BLOCK], wx_ref[j]) for j in range(N_RNN_BLOCKS)],
                         axis=1)
    r = jax.nn.sigmoid(ga + ba_ref[...])
    i = jax.nn.sigmoid(gx + bx_ref[...])
    nl = -lam_ref[...]
    softplus = jnp.maximum(nl, 0.0) + jnp.log1p(jnp.exp(-jnp.abs(nl)))
    log_a = (-RG_C) * r * softplus
    a = jnp.exp(log_a)
    u = jnp.sqrt(-jnp.tanh(log_a) * (a * a + 1.0)) * (i * xc)

    for d in (1, 2, 4):
        m = tpos >= d
        u = jnp.where(m, u + a * pltpu.roll(u, d, axis=0), u)
        a = jnp.where(m, a * pltpu.roll(a, d, axis=0), a)

    gate = _gelu_tanh(gr_ref[...])
    if carry:
        a_s[...] = a
        u_s[...] = u

        def body(g, hc):
            rows = pl.ds(pl.multiple_of(g * SUBLANES, SUBLANES), SUBLANES)
            h = u_s[rows, :] + a_s[rows, :] * hc
            u_s[rows, :] = h
            return jnp.broadcast_to(h[SUBLANES - 1:SUBLANES, :], h.shape)

        hc = lax.fori_loop(0, n // SUBLANES, body, hc_ref[...])
        hc_ref[...] = hc
        y_ref[...] = (u_s[...] * gate).astype(y_ref.dtype)
        hl_ref[...] = hc[:1]
    else:
        h = u + a * h0_ref[...]
        hs_ref[...] = h
        y_ref[...] = (h * gate).astype(y_ref.dtype)


def _rg_weight_specs(imap):
    row = pl.BlockSpec((1, D_RNN), imap(0, 0))
    blk = pl.BlockSpec((N_RNN_BLOCKS, RNN_BLOCK, RNN_BLOCK), imap(0, 0, 0))
    return [pl.BlockSpec((CONV_W, D_RNN), imap(0, 0)), row, blk, row, blk, row, row]


def _rglru_prompt(z, rgw):
    nt = SEQ // RG_TC
    const = lambda *idx: (lambda b, t: idx)
    return pl.pallas_call(
        functools.partial(_rglru_kernel, carry=True),
        grid=(BATCH, nt),
        in_specs=[pl.BlockSpec((RG_TC, D_RNN), lambda b, t: (b * nt + t, COL_XR)),
                  pl.BlockSpec((RG_TC, D_RNN), lambda b, t: (b * nt + t, COL_GR))] + _rg_weight_specs(const),
        out_specs=[pl.BlockSpec((RG_TC, D_RNN), lambda b, t: (b * nt + t, 0)),
                   pl.BlockSpec((None, 1, D_RNN), lambda b, t: (b, 0, 0))],
        out_shape=[jax.ShapeDtypeStruct((TP, D_RNN), BF16), jax.ShapeDtypeStruct((BATCH, 1, D_RNN), F32)],
        scratch_shapes=[pltpu.VMEM((SUBLANES, D_RNN), F32), pltpu.VMEM((SUBLANES, D_RNN), F32),
                        pltpu.VMEM((RG_TC, D_RNN), F32), pltpu.VMEM((RG_TC, D_RNN), F32)],
        compiler_params=_cparams(("arbitrary", "arbitrary")),
        name="rglru_prompt",
    )(z, z, *rgw)


def _rglru_sample(z, hist, h0, rgw):
    nt = TS // RG_TC
    row0 = TP // RG_TC
    const = lambda *idx: (lambda t: idx)
    return pl.pallas_call(
        functools.partial(_rglru_kernel, carry=False),
        grid=(nt,),
        in_specs=[pl.BlockSpec((RG_TC, D_RNN), lambda t: (row0 + t, COL_XR)),
                  pl.BlockSpec((RG_TC, D_RNN), lambda t: (row0 + t, COL_GR)),
                  pl.BlockSpec((RG_TC, D_RNN), lambda t: (t, 0)),
                  pl.BlockSpec((RG_TC, D_RNN), lambda t: (t, 0))] + _rg_weight_specs(const),
        out_specs=[pl.BlockSpec((RG_TC, D_RNN), lambda t: (t, 0)),
                   pl.BlockSpec((RG_TC, D_RNN), lambda t: (t, 0))],
        out_shape=[jax.ShapeDtypeStruct((TS, D_RNN), BF16), jax.ShapeDtypeStruct((TS, D_RNN), F32)],
        compiler_params=_cparams(("arbitrary",)),
        name="rglru_sample",
    )(z, z, hist, h0, *rgw)


MIX_TM = 1024
MIX_TN = 512


def _mix_kernel(oa_ref, yr_ref, wa_ref, wr_ref, ga_ref, gb_ref, o_ref):
    ua = _dot(oa_ref[...], wa_ref[...])
    ur = _dot(yr_ref[...], wr_ref[...])
    o_ref[...] = (jax.nn.sigmoid(ga_ref[...]) * ua + jax.nn.sigmoid(gb_ref[...]) * ur).astype(o_ref.dtype)


def _mix(o_attn, y_rnn, w_attn16, w_rnn16, z, layer):
    nj = D_MODEL // MIX_TN
    return pl.pallas_call(
        _mix_kernel,
        grid=(nj, T // MIX_TM),
        in_specs=[
            pl.BlockSpec((MIX_TM, D_MODEL), lambda j, i: (i, 0)),
            pl.BlockSpec((MIX_TM, D_MODEL), lambda j, i: (i, 0)),
            pl.BlockSpec((None, D_MODEL, MIX_TN), lambda j, i: (layer, 0, j)),
            pl.BlockSpec((None, D_MODEL, MIX_TN), lambda j, i: (layer, 0, j)),
            pl.BlockSpec((MIX_TM, MIX_TN), lambda j, i: (i, COL_GA * nj + j)),
            pl.BlockSpec((MIX_TM, MIX_TN), lambda j, i: (i, COL_GB * nj + j)),
        ],
        out_specs=pl.BlockSpec((MIX_TM, MIX_TN), lambda j, i: (i, j)),
        out_shape=jax.ShapeDtypeStruct((T, D_MODEL), BF16),
        compiler_params=_cparams(("arbitrary", "arbitrary")),
        name="mix",
    )(o_attn, y_rnn, w_attn16, w_rnn16, z, z)


OUT_TM = 512


def _out_ln_kernel(m_ref, w_ref, x_ref, g_ref, b_ref, o32_ref, o16_ref):
    y = _layer_norm(ALPHA * x_ref[...] + _dot(m_ref[...], w_ref[...]), g_ref[...], b_ref[...])
    o32_ref[...] = y
    o16_ref[...] = y.astype(BF16)


def _out_ln(mixed, w_out16, x32, g, b, layer):
    row = pl.BlockSpec((None, 1, D_MODEL), lambda i: (layer, 0, 0))
    return pl.pallas_call(
        _out_ln_kernel,
        grid=(T // OUT_TM,),
        in_specs=[
            pl.BlockSpec((OUT_TM, D_MODEL), lambda i: (i, 0)),
            pl.BlockSpec((None, D_MODEL, D_MODEL), lambda i: (layer, 0, 0)),
            pl.BlockSpec((OUT_TM, D_MODEL), lambda i: (i, 0)),
            row, row,
        ],
        out_specs=[pl.BlockSpec((OUT_TM, D_MODEL), lambda i: (i, 0)),
                   pl.BlockSpec((OUT_TM, D_MODEL), lambda i: (i, 0))],
        out_shape=[jax.ShapeDtypeStruct((T, D_MODEL), F32), jax.ShapeDtypeStruct((T, D_MODEL), BF16)],
        compiler_params=_cparams(("arbitrary",)),
        name="out_ln",
    )(mixed, w_out16, x32, g, b)


RT_TM = 512
META_ROWS = SUBLANES


def _router_kernel(x_ref, w_ref, b_ref, meta_ref, wts_ref, cnt_ref, run_ref):
    i = pl.program_id(0)

    @pl.when(i == 0)
    def _():
        run_ref[...] = jnp.zeros(run_ref.shape, F32)

    logits = lax.dot_general(w_ref[...], x_ref[...], (((1,), (1,)), ((), ())),
                             precision=lax.Precision.HIGHEST, preferred_element_type=F32) + b_ref[...]
    mx = jnp.max(logits, axis=0, keepdims=True)
    e = jnp.exp(logits - mx)
    probs = e / jnp.sum(e, axis=0, keepdims=True)
    rows = [probs[j:j + 1, :] for j in range(N_EXPERTS)]

    best_score = None
    for g in range(N_GROUPS):
        v = rows[g * EXPERTS_PER_GROUP:(g + 1) * EXPERTS_PER_GROUP]
        m1 = functools.reduce(jnp.maximum, v)
        i1 = jnp.full(m1.shape, EXPERTS_PER_GROUP - 1, jnp.int32)
        for j in reversed(range(EXPERTS_PER_GROUP)):
            i1 = jnp.where(v[j] == m1, j, i1)
        rest = [jnp.where(i1 == j, -1.0, v[j]) for j in range(EXPERTS_PER_GROUP)]
        m2 = functools.reduce(jnp.maximum, rest)
        i2 = jnp.full(m1.shape, EXPERTS_PER_GROUP - 1, jnp.int32)
        for j in reversed(range(EXPERTS_PER_GROUP)):
            i2 = jnp.where(rest[j] == m2, j, i2)
        score = m1 + m2
        if best_score is None:
            best_score, b_m1, b_m2, b_e1, b_e2 = score, m1, m2, i1, i2
        else:
            better = score > best_score
            best_score = jnp.where(better, score, best_score)
            b_m1 = jnp.where(better, m1, b_m1)
            b_m2 = jnp.where(better, m2, b_m2)
            b_e1 = jnp.where(better, g * EXPERTS_PER_GROUP + i1, b_e1)
            b_e2 = jnp.where(better, g * EXPERTS_PER_GROUP + i2, b_e2)
    denom = b_m1 + b_m2
    w1 = b_m1 / denom
    w2 = b_m2 / denom

    eid = lax.broadcasted_iota(jnp.int32, (N_EXPERTS, RT_TM), 0)
    oh1 = eid == b_e1
    oh2 = eid == b_e2
    member = jnp.where(oh1 | oh2, 1.0, 0.0)
    earlier = jnp.where(lax.broadcasted_iota(jnp.int32, (RT_TM, RT_TM), 0)
                        < lax.broadcasted_iota(jnp.int32, (RT_TM, RT_TM), 1), 1.0, 0.0).astype(BF16)
    prefix = _dot(member.astype(BF16), earlier) + run_ref[:, :1]
    rank1 = jnp.sum(jnp.where(oh1, prefix, 0.0), axis=0, keepdims=True)
    rank2 = jnp.sum(jnp.where(oh2, prefix, 0.0), axis=0, keepdims=True)
    run_ref[...] = run_ref[...] + jnp.sum(member, axis=1, keepdims=True)
    cnt_ref[...] = run_ref[...]

    msub = lax.broadcasted_iota(jnp.int32, (META_ROWS, RT_TM), 0)
    meta_ref[...] = (jnp.where(msub == 0, b_e1, 0) + jnp.where(msub == 1, b_e2, 0)
                     + jnp.where(msub == 2, rank1.astype(jnp.int32), 0)
                     + jnp.where(msub == 3, rank2.astype(jnp.int32), 0))
    sub = lax.broadcasted_iota(jnp.int32, (LANES, RT_TM), 0)
    wts_t = jnp.where(sub == 0, w1, 0.0) + jnp.where(sub == 1, w2, 0.0)
    wts_ref[...] = wts_t.T


def _router(x32, router_w_t, router_b_col):
    return pl.pallas_call(
        _router_kernel,
        grid=(T // RT_TM,),
        in_specs=[
            pl.BlockSpec((RT_TM, D_MODEL), lambda i: (i, 0)),
            pl.BlockSpec((N_EXPERTS, D_MODEL), lambda i: (0, 0)),
            pl.BlockSpec((N_EXPERTS, 1), lambda i: (0, 0)),
        ],
        out_specs=[pl.BlockSpec((META_ROWS, RT_TM), lambda i: (0, i)),
                   pl.BlockSpec((RT_TM, LANES), lambda i: (i, 0)),
                   pl.BlockSpec((N_EXPERTS, LANES), lambda i: (0, 0))],
        out_shape=[jax.ShapeDtypeStruct((META_ROWS, T), jnp.int32),
                   jax.ShapeDtypeStruct((T, LANES), F32),
                   jax.ShapeDtypeStruct((N_EXPERTS, LANES), F32)],
        scratch_shapes=[pltpu.VMEM((N_EXPERTS, LANES), F32)],
        compiler_params=_cparams(("arbitrary",)),
        name="router",
    )(x32, router_w_t, router_b_col)


GATHER_UNROLL = 8


def _row_copy(src_hbm, src_row, dst_buf, slot, dst_row, sem):
    return pltpu.make_async_copy(src_hbm.at[pl.ds(src_row, 1), :], dst_buf.at[slot, pl.ds(dst_row, 1), :], sem.at[slot])


def _gather_start(idx_ref, base, n_rows, src_hbm, dst_buf, slot, sem):
    def body(r, carry):
        _row_copy(src_hbm, idx_ref[base + r], dst_buf, slot, r, sem).start()
        return carry

    lax.fori_loop(0, n_rows, body, 0, unroll=GATHER_UNROLL)


def _gather_wait(n_rows, src_hbm, dst_buf, slot, sem):
    def body(r, carry):
        _row_copy(src_hbm, 0, dst_buf, slot, r, sem).wait()
        return carry

    lax.fori_loop(0, n_rows, body, 0, unroll=GATHER_UNROLL)


MOE_TM = 256
MOE_NT = 2 * T // MOE_TM + N_EXPERTS
MOE_ROWS = MOE_NT * MOE_TM


def _moe_kernel(te_ref, nused_ref, src_ref, x_hbm, wg_ref, wu_ref, wd_ref, y_ref, buf, sem):
    i = pl.program_id(0)
    n_used = nused_ref[0]

    @pl.when(i == 0)
    def _():
        _gather_start(src_ref, 0, MOE_TM, x_hbm, buf, 0, sem)

    @pl.when(i + 1 < n_used)
    def _():
        _gather_start(src_ref, (i + 1) * MOE_TM, MOE_TM, x_hbm, buf, (i + 1) % 2, sem)

    @pl.when(i < n_used)
    def _():
        slot = i % 2
        _gather_wait(MOE_TM, x_hbm, buf, slot, sem)
        x = buf[slot].astype(BF16)
        he = (jax.nn.silu(_dot(x, wg_ref[...])) * _dot(x, wu_ref[...])).astype(BF16)
        y_ref[...] = _dot(he, wd_ref[...])

    @pl.when(i >= n_used)
    def _():
        y_ref[...] = jnp.zeros(y_ref.shape, F32)


def _moe(tile_expert, n_used, src_token, x32, wg16, wu16, wd16, layer):
    grid_spec = pltpu.PrefetchScalarGridSpec(
        num_scalar_prefetch=3,
        grid=(MOE_NT,),
        in_specs=[
            pl.BlockSpec(memory_space=pl.ANY),
            pl.BlockSpec((None, None, D_MODEL, D_EXPERT), lambda i, te, nu, src: (layer, te[i], 0, 0)),
            pl.BlockSpec((None, None, D_MODEL, D_EXPERT), lambda i, te, nu, src: (layer, te[i], 0, 0)),
            pl.BlockSpec((None, None, D_EXPERT, D_MODEL), lambda i, te, nu, src: (layer, te[i], 0, 0)),
        ],
        out_specs=pl.BlockSpec((MOE_TM, D_MODEL), lambda i, te, nu, src: (i, 0)),
        scratch_shapes=[pltpu.VMEM((2, MOE_TM, D_MODEL), F32), pltpu.SemaphoreType.DMA((2,))],
    )
    return pl.pallas_call(
        _moe_kernel,
        grid_spec=grid_spec,
        out_shape=jax.ShapeDtypeStruct((MOE_ROWS, D_MODEL), F32),
        compiler_params=_cparams(("arbitrary",)),
        name="moe",
    )(tile_expert, n_used, src_token, x32, wg16, wu16, wd16)


CMB_TM = 256
CMB_NT = T // CMB_TM


def _combine_kernel(pos_ref, y_hbm, x16_ref, x32_ref, wts_ref, p_ref, wpg_ref, wpp_ref, g_ref, b_ref,
                    o32_ref, o16_ref, buf, sem):
    i = pl.program_id(0)
    rows = 2 * CMB_TM

    @pl.when(i == 0)
    def _():
        _gather_start(pos_ref, 0, rows, y_hbm, buf, 0, sem)

    @pl.when(i + 1 < CMB_NT)
    def _():
        _gather_start(pos_ref, (i + 1) * rows, rows, y_hbm, buf, (i + 1) % 2, sem)

    slot = i % 2
    _gather_wait(rows, y_hbm, buf, slot, sem)
    wts = wts_ref[...]
    moe = wts[:, 0:1] * buf[slot, :CMB_TM, :] + wts[:, 1:2] * buf[slot, CMB_TM:, :]
    ple = jax.nn.sigmoid(_dot(x16_ref[...], wpg_ref[...])) * _dot(p_ref[...], wpp_ref[...])
    y = _layer_norm(ALPHA * x32_ref[...] + moe + ple, g_ref[...], b_ref[...])
    o32_ref[...] = y
    o16_ref[...] = y.astype(BF16)


def _combine(pos, y_sorted, x16, x32, wts, p16, w_gate16, w_proj16, g, b, layer):
    row = pl.BlockSpec((None, 1, D_MODEL), lambda i, pos: (layer, 0, 0))
    tile = pl.BlockSpec((CMB_TM, D_MODEL), lambda i, pos: (i, 0))
    grid_spec = pltpu.PrefetchScalarGridSpec(
        num_scalar_prefetch=1,
        grid=(CMB_NT,),
        in_specs=[
            pl.BlockSpec(memory_space=pl.ANY),
            tile, tile,
            pl.BlockSpec((CMB_TM, LANES), lambda i, pos: (i, 0)),
            pl.BlockSpec((None, CMB_TM, PLE_DIM), lambda i, pos: (layer, i, 0)),
            pl.BlockSpec((None, D_MODEL, D_MODEL), lambda i, pos: (layer, 0, 0)),
            pl.BlockSpec((None, PLE_DIM, D_MODEL), lambda i, pos: (layer, 0, 0)),
            row, row,
        ],
        out_specs=[tile, tile],
        scratch_shapes=[pltpu.VMEM((2, 2 * CMB_TM, D_MODEL), F32), pltpu.SemaphoreType.DMA((2,))],
    )
    return pl.pallas_call(
        _combine_kernel,
        grid_spec=grid_spec,
        out_shape=[jax.ShapeDtypeStruct((T, D_MODEL), F32), jax.ShapeDtypeStruct((T, D_MODEL), BF16)],
        compiler_params=_cparams(("arbitrary",)),
        name="combine",
    )(pos, y_sorted, x16, x32, wts, p16, w_gate16, w_proj16, g, b)


def _moe_schedule(meta, cnt):
    e1, e2, r1, r2 = meta[0], meta[1], meta[2], meta[3]
    counts = cnt[:, 0].astype(jnp.int32)
    tiles = (counts + MOE_TM - 1) // MOE_TM
    tile_end = jnp.cumsum(tiles)
    offset = (tile_end - tiles) * MOE_TM
    n_used = tile_end[-1:]
    tile_ids = jnp.arange(MOE_NT, dtype=jnp.int32)
    tile_expert = jnp.minimum(jnp.sum((tile_ids[:, None] >= tile_end[None, :]).astype(jnp.int32), axis=1),
                              N_EXPERTS - 1)
    pos1 = offset[e1] + r1
    pos2 = offset[e2] + r2
    tok = jnp.arange(T, dtype=jnp.int32)
    src_token = jnp.zeros((MOE_ROWS,), jnp.int32).at[pos1].set(tok).at[pos2].set(tok)
    pos = jnp.stack([pos1.reshape(CMB_NT, CMB_TM), pos2.reshape(CMB_NT, CMB_TM)], axis=1).reshape(-1)
    return tile_expert, n_used, src_token, pos


def kernel(x_prompt, x_sample, cache_k, cache_v, state_conv, state_h, page_table, p_prompt, p_sample,
           emb_ln_g, emb_ln_b, w_in, lambda_q1, lambda_k1, lambda_q2, lambda_k2, subln_w,
           conv_w, conv_b, rg_w_a, rg_b_a, rg_w_x, rg_b_x, rg_lambda,
           w_branch_attn, w_branch_rnn, w_out, ln1_g, ln1_b, router_w, router_b,
           moe_w_gate, moe_w_up, moe_w_down, w_ple_gate, w_ple_proj, ln2_g, ln2_b):
    row = lambda a: a.reshape(1, -1)
    rows3 = lambda a: a.reshape(DEPTH, 1, -1)

    w_in16 = w_in.astype(BF16)
    w_attn16 = w_branch_attn.astype(BF16)
    w_rnn16 = w_branch_rnn.astype(BF16)
    w_out16 = w_out.astype(BF16)
    w_pg16 = w_ple_gate.astype(BF16)
    w_pp16 = w_ple_proj.astype(BF16)
    wg16 = moe_w_gate.astype(BF16)
    wu16 = moe_w_up.astype(BF16)
    wd16 = moe_w_down.astype(BF16)
    rg_wa16 = rg_w_a.astype(BF16)
    rg_wx16 = rg_w_x.astype(BF16)
    router_w_t = router_w.T
    router_b_col = router_b.reshape(N_EXPERTS, 1)
    ln1_g3, ln1_b3, ln2_g3, ln2_b3 = rows3(ln1_g), rows3(ln1_b), rows3(ln2_g), rows3(ln2_b)
    p16 = jnp.concatenate([p_prompt.reshape(DEPTH, TP, PLE_DIM), p_sample.reshape(DEPTH, TS, PLE_DIM)],
                          axis=1).astype(BF16)

    x32, x16 = _embed_ln(x_prompt.reshape(TP, D_MODEL), x_sample.reshape(TS, D_MODEL), row(emb_ln_g), row(emb_ln_b))

    outs = {k: [] for k in ("kp", "vp", "cp", "hp", "ks", "vs", "cs", "hs")}
    for l in range(DEPTH):
        lam_init = 0.8 - 0.6 * math.exp(-0.3 * l)
        lam_vecs = (row(lambda_q1[l]), row(lambda_k1[l]), row(lambda_q2[l]), row(lambda_k2[l]))
        subln = row(subln_w[l])
        rgw = (conv_w[l], row(conv_b[l]), rg_wa16[l], row(rg_b_a[l]), rg_wx16[l], row(rg_b_x[l]), row(rg_lambda[l]))

        z = _in_proj(x16, w_in16, l)

        o_p = _attn_prompt(z, lam_vecs, subln, lam_init)
        o_s = _attn_sample(z, cache_k, cache_v, page_table, lam_vecs, subln, lam_init, l)
        o_attn = jnp.concatenate([o_p, o_s.astype(BF16)], axis=0)

        y_p, h_p = _rglru_prompt(z, rgw)
        hist = jnp.pad(state_conv[l], ((0, 0), (SUBLANES - (CONV_W - 1), 0), (0, 0))).reshape(TS, D_RNN)
        h0 = jnp.repeat(state_h[l], DEC_SEQ, axis=0)
        y_s, hs_s = _rglru_sample(z, hist, h0, rgw)
        y_rnn = jnp.concatenate([y_p, y_s], axis=0)

        mixed = _mix(o_attn, y_rnn, w_attn16, w_rnn16, z, l)
        x32, x16 = _out_ln(mixed, w_out16, x32, ln1_g3, ln1_b3, l)
        meta, wts, cnt = _router(x32, router_w_t, router_b_col)
        tile_expert, n_used, src_token, pos = _moe_schedule(meta, cnt)
        y_sorted = _moe(tile_expert, n_used, src_token, x32, wg16, wu16, wd16, l)
        x32, x16 = _combine(pos, y_sorted, x16, x32, wts, p16, w_pg16, w_pp16, ln2_g3, ln2_b3, l)

        kv = z[:, COL_K * D_MODEL:(COL_V + 1) * D_MODEL]
        xr = z[:, COL_XR * D_MODEL:(COL_XR + 1) * D_MODEL]
        outs["kp"].append(kv[:TP, :D_MODEL].reshape(BATCH, SEQ, N_HEADS, HEAD_W))
        outs["vp"].append(kv[:TP, D_MODEL:].reshape(BATCH, SEQ, N_HEADS, HEAD_W))
        outs["ks"].append(kv[TP:, :D_MODEL].reshape(DEC_BATCH, DEC_SEQ, N_HEADS, HEAD_W))
        outs["vs"].append(kv[TP:, D_MODEL:].reshape(DEC_BATCH, DEC_SEQ, N_HEADS, HEAD_W))
        outs["cp"].append(xr[:TP].reshape(BATCH, SEQ, D_RNN)[:, SEQ - (CONV_W - 1):])
        outs["cs"].append(xr[TP:].reshape(DEC_BATCH, DEC_SEQ, D_RNN)[:, DEC_SEQ - (CONV_W - 1):])
        outs["hp"].append(h_p.reshape(BATCH, D_RNN))
        outs["hs"].append(hs_s.reshape(DEC_BATCH, DEC_SEQ, D_RNN)[:, DEC_SEQ - 1])

    st = {k: jnp.stack(v) for k, v in outs.items()}
    return (x32[:TP].reshape(BATCH, SEQ, D_MODEL), x32[TP:].reshape(DEC_BATCH, DEC_SEQ, D_MODEL),
            st["kp"], st["vp"], st["cp"], st["hp"], st["ks"], st["vs"], st["cs"], st["hs"])
```

```python
import functools
import math

import jax
import jax.numpy as jnp
from jax import lax
from jax.experimental import pallas as pl
from jax.experimental.pallas import tpu as pltpu

F32 = jnp.float32
BF16 = jnp.bfloat16

D_MODEL = 2048
BATCH = 4
SEQ = 2048
DEPTH = 2
DEC_BATCH = 128
DEC_SEQ = 8
PAST_LEN = 2048
PAGE_SIZE = 128
N_PAGES = PAST_LEN // PAGE_SIZE
HEAD_DIM = 128
N_HEADS = D_MODEL // (2 * HEAD_DIM)
HEAD_W = 2 * HEAD_DIM
D_RNN = D_MODEL
N_RNN_BLOCKS = 8
RNN_BLOCK = D_RNN // N_RNN_BLOCKS
CONV_W = 4
RG_C = 8.0
N_EXPERTS = 16
N_GROUPS = 4
EXPERTS_PER_GROUP = N_EXPERTS // N_GROUPS
D_EXPERT = 1024
PLE_DIM = 256
LN_EPS = 1e-5
N_IN = 7 * D_MODEL
ALPHA = (2.0 * DEPTH) ** 0.25
ATTN_SCALE = HEAD_DIM ** -0.5

TP = BATCH * SEQ
TS = DEC_BATCH * DEC_SEQ
T = TP + TS

COL_Q, COL_K, COL_V, COL_XR, COL_GR, COL_GA, COL_GB = range(7)

SUBLANES = 8
LANES = 128
VMEM_LIMIT = 56 * 1024 * 1024
NEG_BIG = -1e30
LOG2E = math.log2(math.e)


def _cparams(sem):
    return pltpu.CompilerParams(dimension_semantics=sem, vmem_limit_bytes=VMEM_LIMIT)


def _layer_norm(x, g, b):
    mu = jnp.mean(x, axis=-1, keepdims=True)
    xc = x - mu
    var = jnp.mean(xc * xc, axis=-1, keepdims=True)
    return xc * lax.rsqrt(var + LN_EPS) * g + b


def _dot(a, b):
    return jnp.dot(a, b, preferred_element_type=F32)


def _dot_nt(a, b):
    return lax.dot_general(a, b, (((1,), (1,)), ((), ())), preferred_element_type=F32)


def _diff_lambda(lq1, lk1, lq2, lk2, lam_init):
    return (jnp.exp(jnp.sum(lq1 * lk1, axis=-1, keepdims=True))
            - jnp.exp(jnp.sum(lq2 * lk2, axis=-1, keepdims=True)) + lam_init)


EMB_TM = 512


def _embed_ln_kernel(xp_ref, xs_ref, g_ref, b_ref, o32_ref, o16_ref):
    i = pl.program_id(0)

    def go(x):
        y = _layer_norm(x, g_ref[...], b_ref[...])
        o32_ref[...] = y
        o16_ref[...] = y.astype(BF16)

    @pl.when(i < TP // EMB_TM)
    def _():
        go(xp_ref[...])

    @pl.when(i >= TP // EMB_TM)
    def _():
        go(xs_ref[...])


def _embed_ln(xp, xs, g, b):
    n_p = TP // EMB_TM
    return pl.pallas_call(
        _embed_ln_kernel,
        grid=(T // EMB_TM,),
        in_specs=[
            pl.BlockSpec((EMB_TM, D_MODEL), lambda i: (jnp.minimum(i, n_p - 1), 0)),
            pl.BlockSpec((EMB_TM, D_MODEL), lambda i: (jnp.maximum(i - n_p, 0), 0)),
            pl.BlockSpec((1, D_MODEL), lambda i: (0, 0)),
            pl.BlockSpec((1, D_MODEL), lambda i: (0, 0)),
        ],
        out_specs=[pl.BlockSpec((EMB_TM, D_MODEL), lambda i: (i, 0)),
                   pl.BlockSpec((EMB_TM, D_MODEL), lambda i: (i, 0))],
        out_shape=[jax.ShapeDtypeStruct((T, D_MODEL), F32), jax.ShapeDtypeStruct((T, D_MODEL), BF16)],
        compiler_params=_cparams(("arbitrary",)),
        name="embed_ln",
    )(xp, xs, g, b)


IN_TM = 512
IN_TN = 1024
CAST_J = 8
CAST_I = 16
CAST_BLOCKS = CAST_J * CAST_I
W_ELEMS = N_EXPERTS * D_MODEL * D_EXPERT
UP_ROWS = W_ELEMS // D_EXPERT // CAST_BLOCKS
DOWN_ROWS = W_ELEMS // D_MODEL // CAST_BLOCKS


def _in_proj_kernel(a_ref, w_ref, wg_ref, wu_ref, wd_ref, o_ref, wg16_ref, wu16_ref, wd16_ref, w16_ref):
    j = pl.program_id(0)
    i = pl.program_id(1)

    @pl.when(i == 0)
    def _():
        w16_ref[...] = w_ref[...].astype(BF16)

    @pl.when((j < CAST_J) & (i < CAST_I))
    def _():
        wg16_ref[...] = wg_ref[...].astype(BF16)
        wu16_ref[...] = wu_ref[...].astype(BF16)
        wd16_ref[...] = wd_ref[...].astype(BF16)

    o_ref[...] = _dot(a_ref[...], w16_ref[...])


def _in_proj(x16, w_in, moe_w_gate, moe_w_up, moe_w_down, layer):
    def cast_idx(j, i):
        return jnp.where(j < CAST_J, j * CAST_I + jnp.minimum(i, CAST_I - 1), CAST_BLOCKS - 1)

    up_in = pl.BlockSpec((None, UP_ROWS, D_EXPERT), lambda j, i: (layer, cast_idx(j, i), 0))
    down_in = pl.BlockSpec((None, DOWN_ROWS, D_MODEL), lambda j, i: (layer, cast_idx(j, i), 0))
    up_out = pl.BlockSpec((UP_ROWS, D_EXPERT), lambda j, i: (cast_idx(j, i), 0))
    down_out = pl.BlockSpec((DOWN_ROWS, D_MODEL), lambda j, i: (cast_idx(j, i), 0))
    z, wg16, wu16, wd16 = pl.pallas_call(
        _in_proj_kernel,
        grid=(N_IN // IN_TN, T // IN_TM),
        in_specs=[
            pl.BlockSpec((IN_TM, D_MODEL), lambda j, i: (i, 0)),
            pl.BlockSpec((None, D_MODEL, IN_TN), lambda j, i: (layer, 0, j)),
            up_in, up_in, down_in,
        ],
        out_specs=[pl.BlockSpec((IN_TM, IN_TN), lambda j, i: (i, j)), up_out, up_out, down_out],
        out_shape=[jax.ShapeDtypeStruct((T, N_IN), F32),
                   jax.ShapeDtypeStruct((W_ELEMS // D_EXPERT, D_EXPERT), BF16),
                   jax.ShapeDtypeStruct((W_ELEMS // D_EXPERT, D_EXPERT), BF16),
                   jax.ShapeDtypeStruct((W_ELEMS // D_MODEL, D_MODEL), BF16)],
        scratch_shapes=[pltpu.VMEM((D_MODEL, IN_TN), BF16)],
        compiler_params=_cparams(("arbitrary", "arbitrary")),
        name="in_proj",
    )(x16, w_in, moe_w_gate.reshape(DEPTH, -1, D_EXPERT), moe_w_up.reshape(DEPTH, -1, D_EXPERT),
      moe_w_down.reshape(DEPTH, -1, D_MODEL))
    return (z, wg16.reshape(N_EXPERTS, D_MODEL, D_EXPERT), wu16.reshape(N_EXPERTS, D_MODEL, D_EXPERT),
            wd16.reshape(N_EXPERTS, D_EXPERT, D_MODEL))


ATT_TQ = 512
ATT_TK = 512
ATT_NQ = SEQ // ATT_TQ


def _attn_tables():
    qi, kj = [], []
    for q in range(ATT_NQ):
        for k in range(q + 1):
            qi.append(q)
            kj.append(k)
    return jnp.asarray(qi, jnp.int32), jnp.asarray(kj, jnp.int32)


def _headnorm_store(o, subln, lam_init):
    ms = jnp.mean(o * o, axis=-1, keepdims=True)
    return o * lax.rsqrt(ms + LN_EPS) * subln * (1.0 - lam_init)


def _attn_prompt_kernel(qi_tab, kj_tab, q_ref, k_ref, v_ref, lq1, lk1, lq2, lk2, subln_ref, o_ref,
                        qs_ref, m_ref, l_ref, acc_ref, *, lam_init):
    s_idx = pl.program_id(2)
    qi = qi_tab[s_idx]
    kj = kj_tab[s_idx]

    @pl.when(kj == 0)
    def _():
        m_ref[...] = jnp.full(m_ref.shape, NEG_BIG, F32)
        l_ref[...] = jnp.zeros(l_ref.shape, F32)
        acc_ref[...] = jnp.zeros(acc_ref.shape, F32)
        qs_ref[...] = (q_ref[...] * (ATTN_SCALE * LOG2E)).astype(BF16)

    def step(diagonal):
        k = k_ref[...].astype(BF16)
        v = v_ref[...].astype(BF16)
        if diagonal:
            causal = (lax.broadcasted_iota(jnp.int32, (ATT_TQ, ATT_TK), 1)
                      <= lax.broadcasted_iota(jnp.int32, (ATT_TQ, ATT_TK), 0))
        for c in range(2):
            sl = slice(c * HEAD_DIM, (c + 1) * HEAD_DIM)
            s = _dot_nt(qs_ref[:, sl], k[:, sl])
            if diagonal:
                s = jnp.where(causal, s, NEG_BIG)
            m_prev = m_ref[c]
            m_new = jnp.maximum(m_prev, jnp.max(s, axis=-1, keepdims=True))
            corr = jnp.exp2(m_prev - m_new)
            p = jnp.exp2(s - m_new)
            l_ref[c] = l_ref[c] * corr + jnp.sum(p, axis=-1, keepdims=True)
            acc_ref[c] = acc_ref[c] * corr + _dot(p.astype(BF16), v)
            m_ref[c] = m_new

    @pl.when(kj < qi)
    def _():
        step(False)

    @pl.when(kj == qi)
    def _():
        step(True)
        lam = _diff_lambda(lq1[...], lk1[...], lq2[...], lk2[...], lam_init)
        o = acc_ref[0] / l_ref[0] - lam * (acc_ref[1] / l_ref[1])
        o_ref[...] = _headnorm_store(o, subln_ref[...], lam_init).astype(o_ref.dtype)


def _attn_prompt(z, lam_vecs, subln, lam_init):
    qi_tab, kj_tab = _attn_tables()
    n_steps = qi_tab.shape[0]
    vec = pl.BlockSpec((1, HEAD_DIM), lambda b, h, s, qt, kt: (0, 0))
    grid_spec = pltpu.PrefetchScalarGridSpec(
        num_scalar_prefetch=2,
        grid=(BATCH, N_HEADS, n_steps),
        in_specs=[
            pl.BlockSpec((ATT_TQ, HEAD_W), lambda b, h, s, qt, kt: (b * ATT_NQ + qt[s], COL_Q * N_HEADS + h)),
            pl.BlockSpec((ATT_TK, HEAD_W), lambda b, h, s, qt, kt: (b * ATT_NQ + kt[s], COL_K * N_HEADS + h)),
            pl.BlockSpec((ATT_TK, HEAD_W), lambda b, h, s, qt, kt: (b * ATT_NQ + kt[s], COL_V * N_HEADS + h)),
            vec, vec, vec, vec,
            pl.BlockSpec((1, HEAD_W), lambda b, h, s, qt, kt: (0, 0)),
        ],
        out_specs=pl.BlockSpec((ATT_TQ, HEAD_W), lambda b, h, s, qt, kt: (b * ATT_NQ + qt[s], h)),
        scratch_shapes=[pltpu.VMEM((ATT_TQ, HEAD_W), BF16),
                        pltpu.VMEM((2, ATT_TQ, 1), F32), pltpu.VMEM((2, ATT_TQ, 1), F32),
                        pltpu.VMEM((2, ATT_TQ, HEAD_W), F32)],
    )
    return pl.pallas_call(
        functools.partial(_attn_prompt_kernel, lam_init=lam_init),
        grid_spec=grid_spec,
        out_shape=jax.ShapeDtypeStruct((TP, D_MODEL), BF16),
        compiler_params=_cparams(("arbitrary", "arbitrary", "arbitrary")),
        name="attn_prompt",
    )(qi_tab, kj_tab, z, z, z, *lam_vecs, subln)


QROWS = N_HEADS * 2 * DEC_SEQ
HROWS = 2 * DEC_SEQ
KROWS = PAGE_SIZE * N_HEADS
NROWS = DEC_SEQ * N_HEADS
PAGES_PER_STEP = 8
N_PSTEPS = N_PAGES // PAGES_PER_STEP


def _attn_sample_kernel(pt_ref, q_ref, *refs, lam_init):
    kc_refs = refs[:PAGES_PER_STEP]
    vc_refs = refs[PAGES_PER_STEP:2 * PAGES_PER_STEP]
    (kn_ref, vn_ref, lq1, lk1, lq2, lk2, subln_ref, o_ref,
     qb_ref, bias_ref, m_ref, l_ref, acc_ref) = refs[2 * PAGES_PER_STEP:]
    b_idx = pl.program_id(0)
    p_idx = pl.program_id(1)

    @pl.when((b_idx == 0) & (p_idx == 0))
    def _():
        row_head = lax.broadcasted_iota(jnp.int32, (QROWS, KROWS), 0) // HROWS
        col_head = lax.broadcasted_iota(jnp.int32, (QROWS, KROWS), 1) % N_HEADS
        bias_ref[...] = jnp.where(row_head == col_head, 0.0, NEG_BIG)

    @pl.when(p_idx == 0)
    def _():
        m_ref[...] = jnp.full(m_ref.shape, NEG_BIG, F32)
        l_ref[...] = jnp.zeros(l_ref.shape, F32)
        acc_ref[...] = jnp.zeros(acc_ref.shape, F32)
        q = q_ref[...] * (ATTN_SCALE * LOG2E)
        lane = lax.broadcasted_iota(jnp.int32, (DEC_SEQ, HEAD_W), 1)
        for h in range(N_HEADS):
            qh = q[:, h * HEAD_W:(h + 1) * HEAD_W]
            qb = jnp.concatenate([jnp.where(lane < HEAD_DIM, qh, 0.0), jnp.where(lane >= HEAD_DIM, qh, 0.0)], axis=0)
            qb_ref[h * HROWS:(h + 1) * HROWS, :] = qb.astype(BF16)

    def softmax_update(s, pv_fn):
        m_prev = m_ref[...]
        m_new = jnp.maximum(m_prev, jnp.max(s, axis=-1, keepdims=True))
        corr = jnp.exp2(m_prev - m_new)
        p = jnp.exp2(s - m_new)
        l_ref[...] = l_ref[...] * corr + jnp.sum(p, axis=-1, keepdims=True)
        acc_ref[...] = acc_ref[...] * corr + pv_fn(p.astype(BF16))
        m_ref[...] = m_new

    qb = qb_ref[...]
    bias = bias_ref[...]
    s = jnp.concatenate([_dot_nt(qb, kc[...].astype(BF16)) + bias for kc in kc_refs], axis=1)

    def pv_pages(pb):
        out = _dot(pb[:, :KROWS], vc_refs[0][...].astype(BF16))
        for j in range(1, PAGES_PER_STEP):
            out = out + _dot(pb[:, j * KROWS:(j + 1) * KROWS], vc_refs[j][...].astype(BF16))
        return out

    softmax_update(s, pv_pages)

    @pl.when(p_idx == N_PSTEPS - 1)
    def _():
        vb = vn_ref[...].astype(BF16)
        qpos = lax.broadcasted_iota(jnp.int32, (QROWS, NROWS), 0) % DEC_SEQ
        kpos = lax.broadcasted_iota(jnp.int32, (QROWS, NROWS), 1) // N_HEADS
        sn = _dot_nt(qb, kn_ref[...].astype(BF16)) + bias[:, :NROWS]
        sn = jnp.where(kpos <= qpos, sn, NEG_BIG)
        softmax_update(sn, lambda pb: _dot(pb, vb))
        lam = _diff_lambda(lq1[...], lk1[...], lq2[...], lk2[...], lam_init)
        o = acc_ref[...] / l_ref[...]
        for h in range(N_HEADS):
            oh = o[h * HROWS:h * HROWS + DEC_SEQ, :] - lam * o[h * HROWS + DEC_SEQ:(h + 1) * HROWS, :]
            o_ref[:, h * HEAD_W:(h + 1) * HEAD_W] = _headnorm_store(oh, subln_ref[...], lam_init)


def _attn_sample(z, k_new, v_new, cache_k, cache_v, page_table, lam_vecs, subln, lam_init, layer):
    n_pool = cache_k.shape[1]
    kn = k_new.reshape(DEC_BATCH, NROWS, HEAD_W)
    vn = v_new.reshape(DEC_BATCH, NROWS, HEAD_W)
    ck = cache_k.reshape(DEPTH, n_pool, KROWS, HEAD_W)
    cv = cache_v.reshape(DEPTH, n_pool, KROWS, HEAD_W)
    pt = page_table.reshape(-1)
    row0 = TP // DEC_SEQ
    vec = pl.BlockSpec((1, HEAD_DIM), lambda b, p, pt: (0, 0))

    def page_spec(j):
        return pl.BlockSpec((None, None, KROWS, HEAD_W),
                            lambda b, p, pt: (layer, pt[b * N_PAGES + p * PAGES_PER_STEP + j], 0, 0))

    pages = [page_spec(j) for j in range(PAGES_PER_STEP)]
    grid_spec = pltpu.PrefetchScalarGridSpec(
        num_scalar_prefetch=1,
        grid=(DEC_BATCH, N_PSTEPS),
        in_specs=[pl.BlockSpec((DEC_SEQ, D_MODEL), lambda b, p, pt: (row0 + b, COL_Q))] + pages + pages + [
            pl.BlockSpec((None, NROWS, HEAD_W), lambda b, p, pt: (b, 0, 0)),
            pl.BlockSpec((None, NROWS, HEAD_W), lambda b, p, pt: (b, 0, 0)),
            vec, vec, vec, vec,
            pl.BlockSpec((1, HEAD_W), lambda b, p, pt: (0, 0)),
        ],
        out_specs=pl.BlockSpec((DEC_SEQ, D_MODEL), lambda b, p, pt: (b, 0)),
        scratch_shapes=[pltpu.VMEM((QROWS, HEAD_W), BF16), pltpu.VMEM((QROWS, KROWS), F32),
                        pltpu.VMEM((QROWS, 1), F32), pltpu.VMEM((QROWS, 1), F32), pltpu.VMEM((QROWS, HEAD_W), F32)],
    )
    return pl.pallas_call(
        functools.partial(_attn_sample_kernel, lam_init=lam_init),
        grid_spec=grid_spec,
        out_shape=jax.ShapeDtypeStruct((TS, D_MODEL), F32),
        compiler_params=_cparams(("arbitrary", "arbitrary")),
        name="attn_sample",
    )(pt, z, *([ck] * PAGES_PER_STEP), *([cv] * PAGES_PER_STEP), kn, vn, *lam_vecs, subln)


RG_TC = 256


def _gelu_tanh(x):
    return 0.5 * x * (1.0 + jnp.tanh(math.sqrt(2.0 / math.pi) * (x + 0.044715 * (x * x * x))))


def _rglru_kernel(*refs, carry):
    if carry:
        (xr_ref, gr_ref, cw_ref, cb_ref, wa_ref, ba_ref, wx_ref, bx_ref, lam_ref,
         y_ref, hl_ref, xprev_ref, hc_ref, a_s, u_s) = refs
    else:
        (xr_ref, gr_ref, hist_ref, h0_ref, cw_ref, cb_ref, wa_ref, ba_ref, wx_ref, bx_ref, lam_ref,
         y_ref, hs_ref) = refs
    t = pl.program_id(1) if carry else None
    x = xr_ref[...]
    n = x.shape[0]
    tpos = lax.broadcasted_iota(jnp.int32, x.shape, 0) % SUBLANES

    if carry:
        @pl.when(t == 0)
        def _():
            xprev_ref[...] = jnp.zeros(xprev_ref.shape, F32)
            hc_ref[...] = jnp.zeros(hc_ref.shape, F32)

    xc = cb_ref[...] + cw_ref[CONV_W - 1:CONV_W, :] * x
    for s in range(1, CONV_W):
        w_s = cw_ref[CONV_W - 1 - s:CONV_W - s, :]
        if carry:
            xc = xc + w_s * pltpu.roll(x, s, axis=0)
        else:
            hist = pltpu.roll(hist_ref[...], n - SUBLANES + s, axis=0)
            xc = xc + w_s * jnp.where(tpos >= s, pltpu.roll(x, s, axis=0), hist)
    if carry:
        x0 = x[:SUBLANES]
        xp = xprev_ref[...]
        t0 = lax.broadcasted_iota(jnp.int32, x0.shape, 0)
        xc0 = cb_ref[...] + cw_ref[CONV_W - 1:CONV_W, :] * x0
        for s in range(1, CONV_W):
            w_s = cw_ref[CONV_W - 1 - s:CONV_W - s, :]
            xc0 = xc0 + w_s * jnp.where(t0 >= s, pltpu.roll(x0, s, axis=0), pltpu.roll(xp, s, axis=0))
        xc = jnp.concatenate([xc0, xc[SUBLANES:]], axis=0)
        xprev_ref[...] = x[n - SUBLANES:]

    xcb = xc.astype(BF16)
    ga = jnp.concatenate([_dot(xcb[:, j * RNN_BLOCK:(j + 1) * RNN_BLOCK], wa_ref[j]) for j in range(N_RNN_BLOCKS)],
                         axis=1)
    gx = jnp.concatenate([_dot(xcb[:, j * RNN_BLOCK:(j + 1) * RNN_BLOCK], wx_ref[j]) for j in range(N_RNN_BLOCKS)],
                         axis=1)
    r = jax.nn.sigmoid(ga + ba_ref[...])
    i = jax.nn.sigmoid(gx + bx_ref[...])
    nl = -lam_ref[...]
    softplus = jnp.maximum(nl, 0.0) + jnp.log1p(jnp.exp(-jnp.abs(nl)))
    log_a = (-RG_C) * r * softplus
    a = jnp.exp(log_a)
    u = jnp.sqrt(-jnp.tanh(log_a) * (a * a + 1.0)) * (i * xc)

    for d in (1, 2, 4):
        m = tpos >= d
        u = jnp.where(m, u + a * pltpu.roll(u, d, axis=0), u)
        a = jnp.where(m, a * pltpu.roll(a, d, axis=0), a)

    gate = _gelu_tanh(gr_ref[...])
    if carry:
        a_s[...] = a
        u_s[...] = u

        def body(g, hc):
            rows = pl.ds(pl.multiple_of(g * SUBLANES, SUBLANES), SUBLANES)
            h = u_s[rows, :] + a_s[rows, :] * hc
            u_s[rows, :] = h
            return jnp.broadcast_to(h[SUBLANES - 1:SUBLANES, :], h.shape)

        hc = lax.fori_loop(0, n // SUBLANES, body, hc_ref[...])
        hc_ref[...] = hc
        y_ref[...] = (u_s[...] * gate).astype(y_ref.dtype)
        hl_ref[...] = hc[:1]
    else:
        h = u + a * h0_ref[...]
        hs_ref[...] = h
        y_ref[...] = (h * gate).astype(y_ref.dtype)


def _rg_weight_specs(imap):
    row = pl.BlockSpec((1, D_RNN), imap(0, 0))
    blk = pl.BlockSpec((N_RNN_BLOCKS, RNN_BLOCK, RNN_BLOCK), imap(0, 0, 0))
    return [pl.BlockSpec((CONV_W, D_RNN), imap(0, 0)), row, blk, row, blk, row, row]


def _rglru_prompt(z, rgw):
    nt = SEQ // RG_TC
    const = lambda *idx: (lambda b, t: idx)
    return pl.pallas_call(
        functools.partial(_rglru_kernel, carry=True),
        grid=(BATCH, nt),
        in_specs=[pl.BlockSpec((RG_TC, D_RNN), lambda b, t: (b * nt + t, COL_XR)),
                  pl.BlockSpec((RG_TC, D_RNN), lambda b, t: (b * nt + t, COL_GR))] + _rg_weight_specs(const),
        out_specs=[pl.BlockSpec((RG_TC, D_RNN), lambda b, t: (b * nt + t, 0)),
                   pl.BlockSpec((None, 1, D_RNN), lambda b, t: (b, 0, 0))],
        out_shape=[jax.ShapeDtypeStruct((TP, D_RNN), BF16), jax.ShapeDtypeStruct((BATCH, 1, D_RNN), F32)],
        scratch_shapes=[pltpu.VMEM((SUBLANES, D_RNN), F32), pltpu.VMEM((SUBLANES, D_RNN), F32),
                        pltpu.VMEM((RG_TC, D_RNN), F32), pltpu.VMEM((RG_TC, D_RNN), F32)],
        compiler_params=_cparams(("arbitrary", "arbitrary")),
        name="rglru_prompt",
    )(z, z, *rgw)


def _rglru_sample(z, hist, h0, rgw):
    nt = TS // RG_TC
    row0 = TP // RG_TC
    const = lambda *idx: (lambda t: idx)
    return pl.pallas_call(
        functools.partial(_rglru_kernel, carry=False),
        grid=(nt,),
        in_specs=[pl.BlockSpec((RG_TC, D_RNN), lambda t: (row0 + t, COL_XR)),
                  pl.BlockSpec((RG_TC, D_RNN), lambda t: (row0 + t, COL_GR)),
                  pl.BlockSpec((RG_TC, D_RNN), lambda t: (t, 0)),
                  pl.BlockSpec((RG_TC, D_RNN), lambda t: (t, 0))] + _rg_weight_specs(const),
        out_specs=[pl.BlockSpec((RG_TC, D_RNN), lambda t: (t, 0)),
                   pl.BlockSpec((RG_TC, D_RNN), lambda t: (t, 0))],
        out_shape=[jax.ShapeDtypeStruct((TS, D_RNN), BF16), jax.ShapeDtypeStruct((TS, D_RNN), F32)],
        compiler_params=_cparams(("arbitrary",)),
        name="rglru_sample",
    )(z, z, hist, h0, *rgw)


MIX_TM = 512
MIX_TN = 512


def _mix_kernel(oap_ref, yrp_ref, oas_ref, yrs_ref, wa_ref, wr_ref, ga_ref, gb_ref, o_ref):
    i = pl.program_id(1)

    def go(oa_ref, yr_ref):
        ua = _dot(oa_ref[...], wa_ref[...])
        ur = _dot(yr_ref[...], wr_ref[...])
        o_ref[...] = (jax.nn.sigmoid(ga_ref[...]) * ua + jax.nn.sigmoid(gb_ref[...]) * ur).astype(o_ref.dtype)

    @pl.when(i < TP // MIX_TM)
    def _():
        go(oap_ref, yrp_ref)

    @pl.when(i >= TP // MIX_TM)
    def _():
        go(oas_ref, yrs_ref)


def _mix(o_p, y_p, o_s, y_s, w_attn16, w_rnn16, z, layer):
    nj = D_MODEL // MIX_TN
    n_p = TP // MIX_TM
    prompt = pl.BlockSpec((MIX_TM, D_MODEL), lambda j, i: (jnp.minimum(i, n_p - 1), 0))
    sample = pl.BlockSpec((MIX_TM, D_MODEL), lambda j, i: (jnp.maximum(i - n_p, 0), 0))
    return pl.pallas_call(
        _mix_kernel,
        grid=(nj, T // MIX_TM),
        in_specs=[
            prompt, prompt, sample, sample,
            pl.BlockSpec((None, D_MODEL, MIX_TN), lambda j, i: (layer, 0, j)),
            pl.BlockSpec((None, D_MODEL, MIX_TN), lambda j, i: (layer, 0, j)),
            pl.BlockSpec((MIX_TM, MIX_TN), lambda j, i: (i, COL_GA * nj + j)),
            pl.BlockSpec((MIX_TM, MIX_TN), lambda j, i: (i, COL_GB * nj + j)),
        ],
        out_specs=pl.BlockSpec((MIX_TM, MIX_TN), lambda j, i: (i, j)),
        out_shape=jax.ShapeDtypeStruct((T, D_MODEL), BF16),
        compiler_params=_cparams(("arbitrary", "arbitrary")),
        name="mix",
    )(o_p, y_p, o_s, y_s, w_attn16, w_rnn16, z, z)


OUT_TM = 512


def _out_ln_kernel(m_ref, w_ref, x_ref, g_ref, b_ref, o32_ref, o16_ref):
    y = _layer_norm(ALPHA * x_ref[...] + _dot(m_ref[...], w_ref[...]), g_ref[...], b_ref[...])
    o32_ref[...] = y
    o16_ref[...] = y.astype(BF16)


def _out_ln(mixed, w_out16, x32, g, b, layer):
    row = pl.BlockSpec((None, 1, D_MODEL), lambda i: (layer, 0, 0))
    return pl.pallas_call(
        _out_ln_kernel,
        grid=(T // OUT_TM,),
        in_specs=[
            pl.BlockSpec((OUT_TM, D_MODEL), lambda i: (i, 0)),
            pl.BlockSpec((None, D_MODEL, D_MODEL), lambda i: (layer, 0, 0)),
            pl.BlockSpec((OUT_TM, D_MODEL), lambda i: (i, 0)),
            row, row,
        ],
        out_specs=[pl.BlockSpec((OUT_TM, D_MODEL), lambda i: (i, 0)),
                   pl.BlockSpec((OUT_TM, D_MODEL), lambda i: (i, 0))],
        out_shape=[jax.ShapeDtypeStruct((T, D_MODEL), F32), jax.ShapeDtypeStruct((T, D_MODEL), BF16)],
        compiler_params=_cparams(("arbitrary",)),
        name="out_ln",
    )(mixed, w_out16, x32, g, b)


RT_TM = 512
META_ROWS = SUBLANES


def _router_kernel(x_ref, w_ref, b_ref, meta_ref, wts_ref, cnt_ref, run_ref):
    i = pl.program_id(0)

    @pl.when(i == 0)
    def _():
        run_ref[...] = jnp.zeros(run_ref.shape, F32)

    logits = lax.dot_general(w_ref[...], x_ref[...], (((1,), (1,)), ((), ())),
                             precision=lax.Precision.HIGHEST, preferred_element_type=F32) + b_ref[...]
    mx = jnp.max(logits, axis=0, keepdims=True)
    e = jnp.exp(logits - mx)
    probs = e / jnp.sum(e, axis=0, keepdims=True)
    rows = [probs[j:j + 1, :] for j in range(N_EXPERTS)]

    best_score = None
    for g in range(N_GROUPS):
        v = rows[g * EXPERTS_PER_GROUP:(g + 1) * EXPERTS_PER_GROUP]
        m1 = functools.reduce(jnp.maximum, v)
        i1 = jnp.full(m1.shape, EXPERTS_PER_GROUP - 1, jnp.int32)
        for j in reversed(range(EXPERTS_PER_GROUP)):
            i1 = jnp.where(v[j] == m1, j, i1)
        rest = [jnp.where(i1 == j, -1.0, v[j]) for j in range(EXPERTS_PER_GROUP)]
        m2 = functools.reduce(jnp.maximum, rest)
        i2 = jnp.full(m1.shape, EXPERTS_PER_GROUP - 1, jnp.int32)
        for j in reversed(range(EXPERTS_PER_GROUP)):
            i2 = jnp.where(rest[j] == m2, j, i2)
        score = m1 + m2
        if best_score is None:
            best_score, b_m1, b_m2, b_e1, b_e2 = score, m1, m2, i1, i2
        else:
            better = score > best_score
            best_score = jnp.where(better, score, best_score)
            b_m1 = jnp.where(better, m1, b_m1)
            b_m2 = jnp.where(better, m2, b_m2)
            b_e1 = jnp.where(better, g * EXPERTS_PER_GROUP + i1, b_e1)
            b_e2 = jnp.where(better, g * EXPERTS_PER_GROUP + i2, b_e2)
    denom = b_m1 + b_m2
    w1 = b_m1 / denom
    w2 = b_m2 / denom

    eid = lax.broadcasted_iota(jnp.int32, (N_EXPERTS, RT_TM), 0)
    oh1 = eid == b_e1
    oh2 = eid == b_e2
    member = jnp.where(oh1 | oh2, 1.0, 0.0)
    earlier = jnp.where(lax.broadcasted_iota(jnp.int32, (RT_TM, RT_TM), 0)
                        < lax.broadcasted_iota(jnp.int32, (RT_TM, RT_TM), 1), 1.0, 0.0).astype(BF16)
    prefix = _dot(member.astype(BF16), earlier) + run_ref[:, :1]
    rank1 = jnp.sum(jnp.where(oh1, prefix, 0.0), axis=0, keepdims=True)
    rank2 = jnp.sum(jnp.where(oh2, prefix, 0.0), axis=0, keepdims=True)
    run_ref[...] = run_ref[...] + jnp.sum(member, axis=1, keepdims=True)
    cnt_ref[...] = run_ref[...]

    msub = lax.broadcasted_iota(jnp.int32, (META_ROWS, RT_TM), 0)
    meta_ref[...] = (jnp.where(msub == 0, b_e1, 0) + jnp.where(msub == 1, b_e2, 0)
                     + jnp.where(msub == 2, rank1.astype(jnp.int32), 0)
                     + jnp.where(msub == 3, rank2.astype(jnp.int32), 0))
    sub = lax.broadcasted_iota(jnp.int32, (LANES, RT_TM), 0)
    wts_t = jnp.where(sub == 0, w1, 0.0) + jnp.where(sub == 1, w2, 0.0)
    wts_ref[...] = wts_t.T


def _router(x32, router_w_t, router_b_col):
    return pl.pallas_call(
        _router_kernel,
        grid=(T // RT_TM,),
        in_specs=[
            pl.BlockSpec((RT_TM, D_MODEL), lambda i: (i, 0)),
            pl.BlockSpec((N_EXPERTS, D_MODEL), lambda i: (0, 0)),
            pl.BlockSpec((N_EXPERTS, 1), lambda i: (0, 0)),
        ],
        out_specs=[pl.BlockSpec((META_ROWS, RT_TM), lambda i: (0, i)),
                   pl.BlockSpec((RT_TM, LANES), lambda i: (i, 0)),
                   pl.BlockSpec((N_EXPERTS, LANES), lambda i: (0, 0))],
        out_shape=[jax.ShapeDtypeStruct((META_ROWS, T), jnp.int32),
                   jax.ShapeDtypeStruct((T, LANES), F32),
                   jax.ShapeDtypeStruct((N_EXPERTS, LANES), F32)],
        scratch_shapes=[pltpu.VMEM((N_EXPERTS, LANES), F32)],
        compiler_params=_cparams(("arbitrary",)),
        name="router",
    )(x32, router_w_t, router_b_col)


GATHER_UNROLL = 8


def _row_copy(src_hbm, src_row, dst_buf, slot, dst_row, sem):
    return pltpu.make_async_copy(src_hbm.at[pl.ds(src_row, 1), :], dst_buf.at[slot, pl.ds(dst_row, 1), :], sem.at[slot])


def _gather_start(idx_ref, base, n_rows, src_hbm, dst_buf, slot, sem):
    def body(r, carry):
        _row_copy(src_hbm, idx_ref[base + r], dst_buf, slot, r, sem).start()
        return carry

    lax.fori_loop(0, n_rows, body, 0, unroll=GATHER_UNROLL)


def _gather_start_inline(idx_ref, base, n_rows, src_hbm, dst_buf, slot, sem):
    for r in range(n_rows):
        _row_copy(src_hbm, idx_ref[base + r], dst_buf, slot, r, sem).start()


def _gather_wait(n_rows, src_hbm, dst_buf, slot, sem):
    def body(r, carry):
        _row_copy(src_hbm, 0, dst_buf, slot, r, sem).wait()
        return carry

    lax.fori_loop(0, n_rows, body, 0, unroll=GATHER_UNROLL)


MOE_TM = 256
MOE_NT = 2 * T // MOE_TM + N_EXPERTS
MOE_ROWS = MOE_NT * MOE_TM


def _moe_kernel(te_ref, nused_ref, src_ref, x_hbm, wg_ref, wu_ref, wd_ref, y_ref, buf, sem):
    i = pl.program_id(0)
    n_used = nused_ref[0]

    @pl.when(i == 0)
    def _():
        _gather_start(src_ref, 0, MOE_TM, x_hbm, buf, 0, sem)

    @pl.when(i < n_used)
    def _():
        slot = i % 2
        _gather_wait(MOE_TM, x_hbm, buf, slot, sem)
        x = buf[slot].astype(BF16)
        nxt = jnp.minimum(i + 1, n_used - 1)
        _gather_start_inline(src_ref, nxt * MOE_TM, MOE_TM, x_hbm, buf, 1 - slot, sem)
        he = (jax.nn.silu(_dot(x, wg_ref[...])) * _dot(x, wu_ref[...])).astype(BF16)
        y_ref[...] = _dot(he, wd_ref[...])

    @pl.when(i == n_used)
    def _():
        _gather_wait(MOE_TM, x_hbm, buf, i % 2, sem)

    @pl.when(i >= n_used)
    def _():
        y_ref[...] = jnp.zeros(y_ref.shape, F32)


def _moe(tile_expert, n_used, src_token, x32, wg16, wu16, wd16):
    grid_spec = pltpu.PrefetchScalarGridSpec(
        num_scalar_prefetch=3,
        grid=(MOE_NT,),
        in_specs=[
            pl.BlockSpec(memory_space=pl.ANY),
            pl.BlockSpec((None, D_MODEL, D_EXPERT), lambda i, te, nu, src: (te[i], 0, 0)),
            pl.BlockSpec((None, D_MODEL, D_EXPERT), lambda i, te, nu, src: (te[i], 0, 0)),
            pl.BlockSpec((None, D_EXPERT, D_MODEL), lambda i, te, nu, src: (te[i], 0, 0)),
        ],
        out_specs=pl.BlockSpec((MOE_TM, D_MODEL), lambda i, te, nu, src: (i, 0)),
        scratch_shapes=[pltpu.VMEM((2, MOE_TM, D_MODEL), F32), pltpu.SemaphoreType.DMA((2,))],
    )
    return pl.pallas_call(
        _moe_kernel,
        grid_spec=grid_spec,
        out_shape=jax.ShapeDtypeStruct((MOE_ROWS, D_MODEL), F32),
        compiler_params=_cparams(("arbitrary",)),
        name="moe",
    )(tile_expert, n_used, src_token, x32, wg16, wu16, wd16)


CMB_TM = 256
CMB_NT = T // CMB_TM


def _combine_kernel(pos_ref, y_hbm, x16_ref, x32_ref, wts_ref, p_ref, wpg_ref, wpp_ref, g_ref, b_ref,
                    o32_ref, o16_ref, buf, sem):
    i = pl.program_id(0)
    rows = 2 * CMB_TM

    @pl.when(i == 0)
    def _():
        _gather_start(pos_ref, 0, rows, y_hbm, buf, 0, sem)

    slot = i % 2
    _gather_wait(rows, y_hbm, buf, slot, sem)
    nxt = jnp.minimum(i + 1, CMB_NT - 1)
    _gather_start_inline(pos_ref, nxt * rows, rows, y_hbm, buf, 1 - slot, sem)
    wts = wts_ref[...]
    moe = wts[:, 0:1] * buf[slot, :CMB_TM, :] + wts[:, 1:2] * buf[slot, CMB_TM:, :]
    ple = jax.nn.sigmoid(_dot(x16_ref[...], wpg_ref[...])) * _dot(p_ref[...], wpp_ref[...])
    y = _layer_norm(ALPHA * x32_ref[...] + moe + ple, g_ref[...], b_ref[...])
    o32_ref[...] = y
    o16_ref[...] = y.astype(BF16)

    @pl.when(i == CMB_NT - 1)
    def _():
        _gather_wait(rows, y_hbm, buf, 1 - slot, sem)


def _combine(pos, y_sorted, x16, x32, wts, p16, w_gate16, w_proj16, g, b, layer):
    row = pl.BlockSpec((None, 1, D_MODEL), lambda i, pos: (layer, 0, 0))
    tile = pl.BlockSpec((CMB_TM, D_MODEL), lambda i, pos: (i, 0))
    grid_spec = pltpu.PrefetchScalarGridSpec(
        num_scalar_prefetch=1,
        grid=(CMB_NT,),
        in_specs=[
            pl.BlockSpec(memory_space=pl.ANY),
            tile, tile,
            pl.BlockSpec((CMB_TM, LANES), lambda i, pos: (i, 0)),
            pl.BlockSpec((None, CMB_TM, PLE_DIM), lambda i, pos: (layer, i, 0)),
            pl.BlockSpec((None, D_MODEL, D_MODEL), lambda i, pos: (layer, 0, 0)),
            pl.BlockSpec((None, PLE_DIM, D_MODEL), lambda i, pos: (layer, 0, 0)),
            row, row,
        ],
        out_specs=[tile, tile],
        scratch_shapes=[pltpu.VMEM((2, 2 * CMB_TM, D_MODEL), F32), pltpu.SemaphoreType.DMA((2,))],
    )
    return pl.pallas_call(
        _combine_kernel,
        grid_spec=grid_spec,
        out_shape=[jax.ShapeDtypeStruct((T, D_MODEL), F32), jax.ShapeDtypeStruct((T, D_MODEL), BF16)],
        compiler_params=_cparams(("arbitrary",)),
        name="combine",
    )(pos, y_sorted, x16, x32, wts, p16, w_gate16, w_proj16, g, b)


def _moe_schedule(meta, cnt):
    e1, e2, r1, r2 = meta[0], meta[1], meta[2], meta[3]
    counts = cnt[:, 0].astype(jnp.int32)
    tiles = (counts + MOE_TM - 1) // MOE_TM
    tile_end = jnp.cumsum(tiles)
    offset = (tile_end - tiles) * MOE_TM
    n_used = tile_end[-1:]
    tile_ids = jnp.arange(MOE_NT, dtype=jnp.int32)
    tile_expert = jnp.minimum(jnp.sum((tile_ids[:, None] >= tile_end[None, :]).astype(jnp.int32), axis=1),
                              N_EXPERTS - 1)
    pos1 = offset[e1] + r1
    pos2 = offset[e2] + r2
    tok = jnp.arange(T, dtype=jnp.int32)
    src_token = jnp.zeros((MOE_ROWS,), jnp.int32).at[pos1].set(tok).at[pos2].set(tok)
    pos = jnp.stack([pos1.reshape(CMB_NT, CMB_TM), pos2.reshape(CMB_NT, CMB_TM)], axis=1).reshape(-1)
    return tile_expert, n_used, src_token, pos


def kernel(x_prompt, x_sample, cache_k, cache_v, state_conv, state_h, page_table, p_prompt, p_sample,
           emb_ln_g, emb_ln_b, w_in, lambda_q1, lambda_k1, lambda_q2, lambda_k2, subln_w,
           conv_w, conv_b, rg_w_a, rg_b_a, rg_w_x, rg_b_x, rg_lambda,
           w_branch_attn, w_branch_rnn, w_out, ln1_g, ln1_b, router_w, router_b,
           moe_w_gate, moe_w_up, moe_w_down, w_ple_gate, w_ple_proj, ln2_g, ln2_b):
    row = lambda a: a.reshape(1, -1)
    rows3 = lambda a: a.reshape(DEPTH, 1, -1)

    w_attn16 = w_branch_attn.astype(BF16)
    w_rnn16 = w_branch_rnn.astype(BF16)
    w_out16 = w_out.astype(BF16)
    w_pg16 = w_ple_gate.astype(BF16)
    w_pp16 = w_ple_proj.astype(BF16)
    rg_wa16 = rg_w_a.astype(BF16)
    rg_wx16 = rg_w_x.astype(BF16)
    router_w_t = router_w.T
    router_b_col = router_b.reshape(N_EXPERTS, 1)
    ln1_g3, ln1_b3, ln2_g3, ln2_b3 = rows3(ln1_g), rows3(ln1_b), rows3(ln2_g), rows3(ln2_b)
    p16 = jnp.concatenate([p_prompt.reshape(DEPTH, TP, PLE_DIM), p_sample.reshape(DEPTH, TS, PLE_DIM)],
                          axis=1).astype(BF16)

    x32, x16 = _embed_ln(x_prompt.reshape(TP, D_MODEL), x_sample.reshape(TS, D_MODEL), row(emb_ln_g), row(emb_ln_b))

    outs = {k: [] for k in ("kp", "vp", "cp", "hp", "ks", "vs", "cs", "hs")}
    for l in range(DEPTH):
        lam_init = 0.8 - 0.6 * math.exp(-0.3 * l)
        lam_vecs = (row(lambda_q1[l]), row(lambda_k1[l]), row(lambda_q2[l]), row(lambda_k2[l]))
        subln = row(subln_w[l])
        rgw = (conv_w[l], row(conv_b[l]), rg_wa16[l], row(rg_b_a[l]), rg_wx16[l], row(rg_b_x[l]), row(rg_lambda[l]))

        z, wg16, wu16, wd16 = _in_proj(x16, w_in, moe_w_gate, moe_w_up, moe_w_down, l)

        kv = z[:, COL_K * D_MODEL:(COL_V + 1) * D_MODEL]
        xr = z[:, COL_XR * D_MODEL:(COL_XR + 1) * D_MODEL]
        k_s = kv[TP:, :D_MODEL].reshape(DEC_BATCH, DEC_SEQ, N_HEADS, HEAD_W)
        v_s = kv[TP:, D_MODEL:].reshape(DEC_BATCH, DEC_SEQ, N_HEADS, HEAD_W)

        o_p = _attn_prompt(z, lam_vecs, subln, lam_init)
        o_s = _attn_sample(z, k_s, v_s, cache_k, cache_v, page_table, lam_vecs, subln, lam_init, l)

        y_p, h_p = _rglru_prompt(z, rgw)
        hist = jnp.pad(state_conv[l], ((0, 0), (SUBLANES - (CONV_W - 1), 0), (0, 0))).reshape(TS, D_RNN)
        h0 = jnp.repeat(state_h[l], DEC_SEQ, axis=0)
        y_s, hs_s = _rglru_sample(z, hist, h0, rgw)

        mixed = _mix(o_p, y_p, o_s.astype(BF16), y_s, w_attn16, w_rnn16, z, l)
        x32, x16 = _out_ln(mixed, w_out16, x32, ln1_g3, ln1_b3, l)
        meta, wts, cnt = _router(x32, router_w_t, router_b_col)
        tile_expert, n_used, src_token, pos = _moe_schedule(meta, cnt)
        y_sorted = _moe(tile_expert, n_used, src_token, x32, wg16, wu16, wd16)
        x32, x16 = _combine(pos, y_sorted, x16, x32, wts, p16, w_pg16, w_pp16, ln2_g3, ln2_b3, l)

        outs["kp"].append(kv[:TP, :D_MODEL].reshape(BATCH, SEQ, N_HEADS, HEAD_W))
        outs["vp"].append(kv[:TP, D_MODEL:].reshape(BATCH, SEQ, N_HEADS, HEAD_W))
        outs["ks"].append(k_s)
        outs["vs"].append(v_s)
        outs["cp"].append(xr[:TP].reshape(BATCH, SEQ, D_RNN)[:, SEQ - (CONV_W - 1):])
        outs["cs"].append(xr[TP:].reshape(DEC_BATCH, DEC_SEQ, D_RNN)[:, DEC_SEQ - (CONV_W - 1):])
        outs["hp"].append(h_p.reshape(BATCH, D_RNN))
        outs["hs"].append(hs_s.reshape(DEC_BATCH, DEC_SEQ, D_RNN)[:, DEC_SEQ - 1])

    st = {k: jnp.stack(v) for k, v in outs.items()}
    return (x32[:TP].reshape(BATCH, SEQ, D_MODEL), x32[TP:].reshape(DEC_BATCH, DEC_SEQ, D_MODEL),
            st["kp"], st["vp"], st["cp"], st["hp"], st["ks"], st["vs"], st["cs"], st["hs"])
```

```python
import functools
import math

import jax
import jax.numpy as jnp
from jax import lax
from jax.experimental import pallas as pl
from jax.experimental.pallas import tpu as pltpu

F32 = jnp.float32
BF16 = jnp.bfloat16

D_MODEL = 2048
BATCH = 4
SEQ = 2048
DEPTH = 2
DEC_BATCH = 128
DEC_SEQ = 8
PAST_LEN = 2048
PAGE_SIZE = 128
N_PAGES = PAST_LEN // PAGE_SIZE
HEAD_DIM = 128
N_HEADS = D_MODEL // (2 * HEAD_DIM)
HEAD_W = 2 * HEAD_DIM
D_RNN = D_MODEL
N_RNN_BLOCKS = 8
RNN_BLOCK = D_RNN // N_RNN_BLOCKS
CONV_W = 4
RG_C = 8.0
N_EXPERTS = 16
N_GROUPS = 4
EXPERTS_PER_GROUP = N_EXPERTS // N_GROUPS
D_EXPERT = 1024
PLE_DIM = 256
LN_EPS = 1e-5
N_IN = 7 * D_MODEL
ALPHA = (2.0 * DEPTH) ** 0.25
ATTN_SCALE = HEAD_DIM ** -0.5

TP = BATCH * SEQ
TS = DEC_BATCH * DEC_SEQ
T = TP + TS

COL_Q, COL_K, COL_V, COL_XR, COL_GR, COL_GA, COL_GB = range(7)

SUBLANES = 8
LANES = 128
VMEM_LIMIT = 56 * 1024 * 1024
NEG_BIG = -1e30
LOG2E = math.log2(math.e)


def _cparams(sem):
    return pltpu.CompilerParams(dimension_semantics=sem, vmem_limit_bytes=VMEM_LIMIT)


def _layer_norm(x, g, b):
    mu = jnp.mean(x, axis=-1, keepdims=True)
    xc = x - mu
    var = jnp.mean(xc * xc, axis=-1, keepdims=True)
    return xc * lax.rsqrt(var + LN_EPS) * g + b


def _dot(a, b):
    return jnp.dot(a, b, preferred_element_type=F32)


def _dot_nt(a, b):
    return lax.dot_general(a, b, (((1,), (1,)), ((), ())), preferred_element_type=F32)


def _diff_lambda(lq1, lk1, lq2, lk2, lam_init):
    return (jnp.exp(jnp.sum(lq1 * lk1, axis=-1, keepdims=True))
            - jnp.exp(jnp.sum(lq2 * lk2, axis=-1, keepdims=True)) + lam_init)


EMB_TM = 512


def _embed_ln_kernel(xp_ref, xs_ref, g_ref, b_ref, o32_ref, o16_ref):
    i = pl.program_id(0)

    def go(x):
        y = _layer_norm(x, g_ref[...], b_ref[...])
        o32_ref[...] = y
        o16_ref[...] = y.astype(BF16)

    @pl.when(i < TP // EMB_TM)
    def _():
        go(xp_ref[...])

    @pl.when(i >= TP // EMB_TM)
    def _():
        go(xs_ref[...])


def _embed_ln(xp, xs, g, b):
    n_p = TP // EMB_TM
    return pl.pallas_call(
        _embed_ln_kernel,
        grid=(T // EMB_TM,),
        in_specs=[
            pl.BlockSpec((EMB_TM, D_MODEL), lambda i: (jnp.minimum(i, n_p - 1), 0)),
            pl.BlockSpec((EMB_TM, D_MODEL), lambda i: (jnp.maximum(i - n_p, 0), 0)),
            pl.BlockSpec((1, D_MODEL), lambda i: (0, 0)),
            pl.BlockSpec((1, D_MODEL), lambda i: (0, 0)),
        ],
        out_specs=[pl.BlockSpec((EMB_TM, D_MODEL), lambda i: (i, 0)),
                   pl.BlockSpec((EMB_TM, D_MODEL), lambda i: (i, 0))],
        out_shape=[jax.ShapeDtypeStruct((T, D_MODEL), F32), jax.ShapeDtypeStruct((T, D_MODEL), BF16)],
        compiler_params=_cparams(("arbitrary",)),
        name="embed_ln",
    )(xp, xs, g, b)


IN_TM = 512
IN_TN = 1024
CAST_J = 8
CAST_I = 16
CAST_BLOCKS = CAST_J * CAST_I
W_ELEMS = N_EXPERTS * D_MODEL * D_EXPERT
UP_ROWS = W_ELEMS // D_EXPERT // CAST_BLOCKS
DOWN_ROWS = W_ELEMS // D_MODEL // CAST_BLOCKS


def _in_proj_kernel(a_ref, w_ref, wg_ref, wu_ref, wd_ref, o_ref, wg16_ref, wu16_ref, wd16_ref, w16_ref):
    j = pl.program_id(0)
    i = pl.program_id(1)

    @pl.when(i == 0)
    def _():
        w16_ref[...] = w_ref[...].astype(BF16)

    @pl.when((j < CAST_J) & (i < CAST_I))
    def _():
        wg16_ref[...] = wg_ref[...].astype(BF16)
        wu16_ref[...] = wu_ref[...].astype(BF16)
        wd16_ref[...] = wd_ref[...].astype(BF16)

    o_ref[...] = _dot(a_ref[...], w16_ref[...])


def _in_proj(x16, w_in, moe_w_gate, moe_w_up, moe_w_down, layer):
    def cast_idx(j, i):
        return jnp.where(j < CAST_J, j * CAST_I + jnp.minimum(i, CAST_I - 1), CAST_BLOCKS - 1)

    up_in = pl.BlockSpec((None, UP_ROWS, D_EXPERT), lambda j, i: (layer, cast_idx(j, i), 0))
    down_in = pl.BlockSpec((None, DOWN_ROWS, D_MODEL), lambda j, i: (layer, cast_idx(j, i), 0))
    up_out = pl.BlockSpec((UP_ROWS, D_EXPERT), lambda j, i: (cast_idx(j, i), 0))
    down_out = pl.BlockSpec((DOWN_ROWS, D_MODEL), lambda j, i: (cast_idx(j, i), 0))
    z, wg16, wu16, wd16 = pl.pallas_call(
        _in_proj_kernel,
        grid=(N_IN // IN_TN, T // IN_TM),
        in_specs=[
            pl.BlockSpec((IN_TM, D_MODEL), lambda j, i: (i, 0)),
            pl.BlockSpec((None, D_MODEL, IN_TN), lambda j, i: (layer, 0, j)),
            up_in, up_in, down_in,
        ],
        out_specs=[pl.BlockSpec((IN_TM, IN_TN), lambda j, i: (i, j)), up_out, up_out, down_out],
        out_shape=[jax.ShapeDtypeStruct((T, N_IN), F32),
                   jax.ShapeDtypeStruct((W_ELEMS // D_EXPERT, D_EXPERT), BF16),
                   jax.ShapeDtypeStruct((W_ELEMS // D_EXPERT, D_EXPERT), BF16),
                   jax.ShapeDtypeStruct((W_ELEMS // D_MODEL, D_MODEL), BF16)],
        scratch_shapes=[pltpu.VMEM((D_MODEL, IN_TN), BF16)],
        compiler_params=_cparams(("arbitrary", "arbitrary")),
        name="in_proj",
    )(x16, w_in, moe_w_gate.reshape(DEPTH, -1, D_EXPERT), moe_w_up.reshape(DEPTH, -1, D_EXPERT),
      moe_w_down.reshape(DEPTH, -1, D_MODEL))
    return (z, wg16.reshape(N_EXPERTS, D_MODEL, D_EXPERT), wu16.reshape(N_EXPERTS, D_MODEL, D_EXPERT),
            wd16.reshape(N_EXPERTS, D_EXPERT, D_MODEL))


ATT_TQ = 512
ATT_TK = 512
ATT_NQ = SEQ // ATT_TQ


def _attn_tables():
    qi, kj = [], []
    for q in range(ATT_NQ):
        for k in range(q + 1):
            qi.append(q)
            kj.append(k)
    return jnp.asarray(qi, jnp.int32), jnp.asarray(kj, jnp.int32)


def _headnorm_store(o, subln, lam_init):
    ms = jnp.mean(o * o, axis=-1, keepdims=True)
    return o * lax.rsqrt(ms + LN_EPS) * subln * (1.0 - lam_init)


def _attn_prompt_kernel(qi_tab, kj_tab, q_ref, k_ref, v_ref, lq1, lk1, lq2, lk2, subln_ref, o_ref,
                        qs_ref, m_ref, l_ref, acc_ref, *, lam_init):
    s_idx = pl.program_id(2)
    qi = qi_tab[s_idx]
    kj = kj_tab[s_idx]

    @pl.when(kj == 0)
    def _():
        m_ref[...] = jnp.full(m_ref.shape, NEG_BIG, F32)
        l_ref[...] = jnp.zeros(l_ref.shape, F32)
        acc_ref[...] = jnp.zeros(acc_ref.shape, F32)
        qs_ref[...] = (q_ref[...] * (ATTN_SCALE * LOG2E)).astype(BF16)

    def step(diagonal):
        k = k_ref[...].astype(BF16)
        v = v_ref[...].astype(BF16)
        if diagonal:
            causal = (lax.broadcasted_iota(jnp.int32, (ATT_TQ, ATT_TK), 1)
                      <= lax.broadcasted_iota(jnp.int32, (ATT_TQ, ATT_TK), 0))
        for c in range(2):
            sl = slice(c * HEAD_DIM, (c + 1) * HEAD_DIM)
            s = _dot_nt(qs_ref[:, sl], k[:, sl])
            if diagonal:
                s = jnp.where(causal, s, NEG_BIG)
            m_prev = m_ref[c]
            m_new = jnp.maximum(m_prev, jnp.max(s, axis=-1, keepdims=True))
            corr = jnp.exp2(m_prev - m_new)
            p = jnp.exp2(s - m_new)
            l_ref[c] = l_ref[c] * corr + jnp.sum(p, axis=-1, keepdims=True)
            acc_ref[c] = acc_ref[c] * corr + _dot(p.astype(BF16), v)
            m_ref[c] = m_new

    @pl.when(kj < qi)
    def _():
        step(False)

    @pl.when(kj == qi)
    def _():
        step(True)
        lam = _diff_lambda(lq1[...], lk1[...], lq2[...], lk2[...], lam_init)
        o = acc_ref[0] / l_ref[0] - lam * (acc_ref[1] / l_ref[1])
        o_ref[...] = _headnorm_store(o, subln_ref[...], lam_init).astype(o_ref.dtype)


def _attn_prompt(z, lam_vecs, subln, lam_init):
    qi_tab, kj_tab = _attn_tables()
    n_steps = qi_tab.shape[0]
    vec = pl.BlockSpec((1, HEAD_DIM), lambda b, h, s, qt, kt: (0, 0))
    grid_spec = pltpu.PrefetchScalarGridSpec(
        num_scalar_prefetch=2,
        grid=(BATCH, N_HEADS, n_steps),
        in_specs=[
            pl.BlockSpec((ATT_TQ, HEAD_W), lambda b, h, s, qt, kt: (b * ATT_NQ + qt[s], COL_Q * N_HEADS + h)),
            pl.BlockSpec((ATT_TK, HEAD_W), lambda b, h, s, qt, kt: (b * ATT_NQ + kt[s], COL_K * N_HEADS + h)),
            pl.BlockSpec((ATT_TK, HEAD_W), lambda b, h, s, qt, kt: (b * ATT_NQ + kt[s], COL_V * N_HEADS + h)),
            vec, vec, vec, vec,
            pl.BlockSpec((1, HEAD_W), lambda b, h, s, qt, kt: (0, 0)),
        ],
        out_specs=pl.BlockSpec((ATT_TQ, HEAD_W), lambda b, h, s, qt, kt: (b * ATT_NQ + qt[s], h)),
        scratch_shapes=[pltpu.VMEM((ATT_TQ, HEAD_W), BF16),
                        pltpu.VMEM((2, ATT_TQ, 1), F32), pltpu.VMEM((2, ATT_TQ, 1), F32),
                        pltpu.VMEM((2, ATT_TQ, HEAD_W), F32)],
    )
    return pl.pallas_call(
        functools.partial(_attn_prompt_kernel, lam_init=lam_init),
        grid_spec=grid_spec,
        out_shape=jax.ShapeDtypeStruct((TP, D_MODEL), BF16),
        compiler_params=_cparams(("arbitrary", "arbitrary", "arbitrary")),
        name="attn_prompt",
    )(qi_tab, kj_tab, z, z, z, *lam_vecs, subln)


QROWS = N_HEADS * 2 * DEC_SEQ
HROWS = 2 * DEC_SEQ
KROWS = PAGE_SIZE * N_HEADS
NROWS = DEC_SEQ * N_HEADS
PAGES_PER_STEP = 8
N_PSTEPS = N_PAGES // PAGES_PER_STEP


def _attn_sample_kernel(pt_ref, q_ref, *refs, lam_init):
    kc_refs = refs[:PAGES_PER_STEP]
    vc_refs = refs[PAGES_PER_STEP:2 * PAGES_PER_STEP]
    (kn_ref, vn_ref, lq1, lk1, lq2, lk2, subln_ref, o_ref,
     qb_ref, bias_ref, m_ref, l_ref, acc_ref) = refs[2 * PAGES_PER_STEP:]
    b_idx = pl.program_id(0)
    p_idx = pl.program_id(1)

    @pl.when((b_idx == 0) & (p_idx == 0))
    def _():
        row_head = lax.broadcasted_iota(jnp.int32, (QROWS, KROWS), 0) // HROWS
        col_head = lax.broadcasted_iota(jnp.int32, (QROWS, KROWS), 1) % N_HEADS
        bias_ref[...] = jnp.where(row_head == col_head, 0.0, NEG_BIG)

    @pl.when(p_idx == 0)
    def _():
        m_ref[...] = jnp.full(m_ref.shape, NEG_BIG, F32)
        l_ref[...] = jnp.zeros(l_ref.shape, F32)
        acc_ref[...] = jnp.zeros(acc_ref.shape, F32)
        q = q_ref[...] * (ATTN_SCALE * LOG2E)
        lane = lax.broadcasted_iota(jnp.int32, (DEC_SEQ, HEAD_W), 1)
        for h in range(N_HEADS):
            qh = q[:, h * HEAD_W:(h + 1) * HEAD_W]
            qb = jnp.concatenate([jnp.where(lane < HEAD_DIM, qh, 0.0), jnp.where(lane >= HEAD_DIM, qh, 0.0)], axis=0)
            qb_ref[h * HROWS:(h + 1) * HROWS, :] = qb.astype(BF16)

    def softmax_update(s, pv_fn):
        m_prev = m_ref[...]
        m_new = jnp.maximum(m_prev, jnp.max(s, axis=-1, keepdims=True))
        corr = jnp.exp2(m_prev - m_new)
        p = jnp.exp2(s - m_new)
        l_ref[...] = l_ref[...] * corr + jnp.sum(p, axis=-1, keepdims=True)
        acc_ref[...] = acc_ref[...] * corr + pv_fn(p.astype(BF16))
        m_ref[...] = m_new

    qb = qb_ref[...]
    bias = bias_ref[...]
    s = jnp.concatenate([_dot_nt(qb, kc[...].astype(BF16)) + bias for kc in kc_refs], axis=1)

    def pv_pages(pb):
        out = _dot(pb[:, :KROWS], vc_refs[0][...].astype(BF16))
        for j in range(1, PAGES_PER_STEP):
            out = out + _dot(pb[:, j * KROWS:(j + 1) * KROWS], vc_refs[j][...].astype(BF16))
        return out

    softmax_update(s, pv_pages)

    @pl.when(p_idx == N_PSTEPS - 1)
    def _():
        vb = vn_ref[...].astype(BF16)
        qpos = lax.broadcasted_iota(jnp.int32, (QROWS, NROWS), 0) % DEC_SEQ
        kpos = lax.broadcasted_iota(jnp.int32, (QROWS, NROWS), 1) // N_HEADS
        sn = _dot_nt(qb, kn_ref[...].astype(BF16)) + bias[:, :NROWS]
        sn = jnp.where(kpos <= qpos, sn, NEG_BIG)
        softmax_update(sn, lambda pb: _dot(pb, vb))
        lam = _diff_lambda(lq1[...], lk1[...], lq2[...], lk2[...], lam_init)
        o = acc_ref[...] / l_ref[...]
        for h in range(N_HEADS):
            oh = o[h * HROWS:h * HROWS + DEC_SEQ, :] - lam * o[h * HROWS + DEC_SEQ:(h + 1) * HROWS, :]
            o_ref[:, h * HEAD_W:(h + 1) * HEAD_W] = _headnorm_store(oh, subln_ref[...], lam_init)


def _attn_sample(z, k_new, v_new, cache_k, cache_v, page_table, lam_vecs, subln, lam_init, layer):
    n_pool = cache_k.shape[1]
    kn = k_new.reshape(DEC_BATCH, NROWS, HEAD_W)
    vn = v_new.reshape(DEC_BATCH, NROWS, HEAD_W)
    ck = cache_k.reshape(DEPTH, n_pool, KROWS, HEAD_W)
    cv = cache_v.reshape(DEPTH, n_pool, KROWS, HEAD_W)
    pt = page_table.reshape(-1)
    row0 = TP // DEC_SEQ
    vec = pl.BlockSpec((1, HEAD_DIM), lambda b, p, pt: (0, 0))

    def page_spec(j):
        return pl.BlockSpec((None, None, KROWS, HEAD_W),
                            lambda b, p, pt: (layer, pt[b * N_PAGES + p * PAGES_PER_STEP + j], 0, 0))

    pages = [page_spec(j) for j in range(PAGES_PER_STEP)]
    grid_spec = pltpu.PrefetchScalarGridSpec(
        num_scalar_prefetch=1,
        grid=(DEC_BATCH, N_PSTEPS),
        in_specs=[pl.BlockSpec((DEC_SEQ, D_MODEL), lambda b, p, pt: (row0 + b, COL_Q))] + pages + pages + [
            pl.BlockSpec((None, NROWS, HEAD_W), lambda b, p, pt: (b, 0, 0)),
            pl.BlockSpec((None, NROWS, HEAD_W), lambda b, p, pt: (b, 0, 0)),
            vec, vec, vec, vec,
            pl.BlockSpec((1, HEAD_W), lambda b, p, pt: (0, 0)),
        ],
        out_specs=pl.BlockSpec((DEC_SEQ, D_MODEL), lambda b, p, pt: (b, 0)),
        scratch_shapes=[pltpu.VMEM((QROWS, HEAD_W), BF16), pltpu.VMEM((QROWS, KROWS), F32),
                        pltpu.VMEM((QROWS, 1), F32), pltpu.VMEM((QROWS, 1), F32), pltpu.VMEM((QROWS, HEAD_W), F32)],
    )
    return pl.pallas_call(
        functools.partial(_attn_sample_kernel, lam_init=lam_init),
        grid_spec=grid_spec,
        out_shape=jax.ShapeDtypeStruct((TS, D_MODEL), F32),
        compiler_params=_cparams(("arbitrary", "arbitrary")),
        name="attn_sample",
    )(pt, z, *([ck] * PAGES_PER_STEP), *([cv] * PAGES_PER_STEP), kn, vn, *lam_vecs, subln)


RG_TC = 256


def _gelu_tanh(x):
    return 0.5 * x * (1.0 + jnp.tanh(math.sqrt(2.0 / math.pi) * (x + 0.044715 * (x * x * x))))


def _rglru_kernel(*refs, carry):
    if carry:
        (xr_ref, gr_ref, cw_ref, cb_ref, wa_ref, ba_ref, wx_ref, bx_ref, lam_ref,
         y_ref, hl_ref, xprev_ref, hc_ref, a_s, u_s) = refs
    else:
        (xr_ref, gr_ref, hist_ref, h0_ref, cw_ref, cb_ref, wa_ref, ba_ref, wx_ref, bx_ref, lam_ref,
         y_ref, hs_ref) = refs
    t = pl.program_id(1) if carry else None
    x = xr_ref[...]
    n = x.shape[0]
    tpos = lax.broadcasted_iota(jnp.int32, x.shape, 0) % SUBLANES

    if carry:
        @pl.when(t == 0)
        def _():
            xprev_ref[...] = jnp.zeros(xprev_ref.shape, F32)
            hc_ref[...] = jnp.zeros(hc_ref.shape, F32)

    xc = cb_ref[...] + cw_ref[CONV_W - 1:CONV_W, :] * x
    for s in range(1, CONV_W):
        w_s = cw_ref[CONV_W - 1 - s:CONV_W - s, :]
        if carry:
            xc = xc + w_s * pltpu.roll(x, s, axis=0)
        else:
            hist = pltpu.roll(hist_ref[...], n - SUBLANES + s, axis=0)
            xc = xc + w_s * jnp.where(tpos >= s, pltpu.roll(x, s, axis=0), hist)
    if carry:
        x0 = x[:SUBLANES]
        xp = xprev_ref[...]
        t0 = lax.broadcasted_iota(jnp.int32, x0.shape, 0)
        xc0 = cb_ref[...] + cw_ref[CONV_W - 1:CONV_W, :] * x0
        for s in range(1, CONV_W):
            w_s = cw_ref[CONV_W - 1 - s:CONV_W - s, :]
            xc0 = xc0 + w_s * jnp.where(t0 >= s, pltpu.roll(x0, s, axis=0), pltpu.roll(xp, s, axis=0))
        xc = jnp.concatenate([xc0, xc[SUBLANES:]], axis=0)
        xprev_ref[...] = x[n - SUBLANES:]

    xcb = xc.astype(BF16)
    ga = jnp.concatenate([_dot(xcb[:, j * RNN_BLOCK:(j + 1) * RNN_BLOCK], wa_ref[j]) for j in range(N_RNN_BLOCKS)],
                         axis=1)
    gx = jnp.concatenate([_dot(xcb[:, j * RNN_BLOCK:(j + 1) * RNN_BLOCK], wx_ref[j]) for j in range(N_RNN_BLOCKS)],
                         axis=1)
    r = jax.nn.sigmoid(ga + ba_ref[...])
    i = jax.nn.sigmoid(gx + bx_ref[...])
    nl = -lam_ref[...]
    softplus = jnp.maximum(nl, 0.0) + jnp.log1p(jnp.exp(-jnp.abs(nl)))
    log_a = (-RG_C) * r * softplus
    a = jnp.exp(log_a)
    u = jnp.sqrt(-jnp.tanh(log_a) * (a * a + 1.0)) * (i * xc)

    for d in (1, 2, 4):
        m = tpos >= d
        u = jnp.where(m, u + a * pltpu.roll(u, d, axis=0), u)
        a = jnp.where(m, a * pltpu.roll(a, d, axis=0), a)

    gate = _gelu_tanh(gr_ref[...])
    if carry:
        a_s[...] = a
        u_s[...] = u

        def body(g, hc):
            rows = pl.ds(pl.multiple_of(g * SUBLANES, SUBLANES), SUBLANES)
            h = u_s[rows, :] + a_s[rows, :] * hc
            u_s[rows, :] = h
            return jnp.broadcast_to(h[SUBLANES - 1:SUBLANES, :], h.shape)

        hc = lax.fori_loop(0, n // SUBLANES, body, hc_ref[...])
        hc_ref[...] = hc
        y_ref[...] = (u_s[...] * gate).astype(y_ref.dtype)
        hl_ref[...] = hc[:1]
    else:
        h = u + a * h0_ref[...]
        hs_ref[...] = h
        y_ref[...] = (h * gate).astype(y_ref.dtype)


def _rg_weight_specs(imap):
    row = pl.BlockSpec((1, D_RNN), imap(0, 0))
    blk = pl.BlockSpec((N_RNN_BLOCKS, RNN_BLOCK, RNN_BLOCK), imap(0, 0, 0))
    return [pl.BlockSpec((CONV_W, D_RNN), imap(0, 0)), row, blk, row, blk, row, row]


def _rglru_prompt(z, rgw):
    nt = SEQ // RG_TC
    const = lambda *idx: (lambda b, t: idx)
    return pl.pallas_call(
        functools.partial(_rglru_kernel, carry=True),
        grid=(BATCH, nt),
        in_specs=[pl.BlockSpec((RG_TC, D_RNN), lambda b, t: (b * nt + t, COL_XR)),
                  pl.BlockSpec((RG_TC, D_RNN), lambda b, t: (b * nt + t, COL_GR))] + _rg_weight_specs(const),
        out_specs=[pl.BlockSpec((RG_TC, D_RNN), lambda b, t: (b * nt + t, 0)),
                   pl.BlockSpec((None, 1, D_RNN), lambda b, t: (b, 0, 0))],
        out_shape=[jax.ShapeDtypeStruct((TP, D_RNN), BF16), jax.ShapeDtypeStruct((BATCH, 1, D_RNN), F32)],
        scratch_shapes=[pltpu.VMEM((SUBLANES, D_RNN), F32), pltpu.VMEM((SUBLANES, D_RNN), F32),
                        pltpu.VMEM((RG_TC, D_RNN), F32), pltpu.VMEM((RG_TC, D_RNN), F32)],
        compiler_params=_cparams(("arbitrary", "arbitrary")),
        name="rglru_prompt",
    )(z, z, *rgw)


def _rglru_sample(z, hist, h0, rgw):
    nt = TS // RG_TC
    row0 = TP // RG_TC
    const = lambda *idx: (lambda t: idx)
    return pl.pallas_call(
        functools.partial(_rglru_kernel, carry=False),
        grid=(nt,),
        in_specs=[pl.BlockSpec((RG_TC, D_RNN), lambda t: (row0 + t, COL_XR)),
                  pl.BlockSpec((RG_TC, D_RNN), lambda t: (row0 + t, COL_GR)),
                  pl.BlockSpec((RG_TC, D_RNN), lambda t: (t, 0)),
                  pl.BlockSpec((RG_TC, D_RNN), lambda t: (t, 0))] + _rg_weight_specs(const),
        out_specs=[pl.BlockSpec((RG_TC, D_RNN), lambda t: (t, 0)),
                   pl.BlockSpec((RG_TC, D_RNN), lambda t: (t, 0))],
        out_shape=[jax.ShapeDtypeStruct((TS, D_RNN), BF16), jax.ShapeDtypeStruct((TS, D_RNN), F32)],
        compiler_params=_cparams(("arbitrary",)),
        name="rglru_sample",
    )(z, z, hist, h0, *rgw)


MIX_TM = 512
MIX_TN = 512


def _mix_kernel(oap_ref, yrp_ref, oas_ref, yrs_ref, wa_ref, wr_ref, ga_ref, gb_ref, o_ref):
    i = pl.program_id(1)

    def go(oa_ref, yr_ref):
        ua = _dot(oa_ref[...], wa_ref[...])
        ur = _dot(yr_ref[...], wr_ref[...])
        o_ref[...] = (jax.nn.sigmoid(ga_ref[...]) * ua + jax.nn.sigmoid(gb_ref[...]) * ur).astype(o_ref.dtype)

    @pl.when(i < TP // MIX_TM)
    def _():
        go(oap_ref, yrp_ref)

    @pl.when(i >= TP // MIX_TM)
    def _():
        go(oas_ref, yrs_ref)


def _mix(o_p, y_p, o_s, y_s, w_attn16, w_rnn16, z, layer):
    nj = D_MODEL // MIX_TN
    n_p = TP // MIX_TM
    prompt = pl.BlockSpec((MIX_TM, D_MODEL), lambda j, i: (jnp.minimum(i, n_p - 1), 0))
    sample = pl.BlockSpec((MIX_TM, D_MODEL), lambda j, i: (jnp.maximum(i - n_p, 0), 0))
    return pl.pallas_call(
        _mix_kernel,
        grid=(nj, T // MIX_TM),
        in_specs=[
            prompt, prompt, sample, sample,
            pl.BlockSpec((None, D_MODEL, MIX_TN), lambda j, i: (layer, 0, j)),
            pl.BlockSpec((None, D_MODEL, MIX_TN), lambda j, i: (layer, 0, j)),
            pl.BlockSpec((MIX_TM, MIX_TN), lambda j, i: (i, COL_GA * nj + j)),
            pl.BlockSpec((MIX_TM, MIX_TN), lambda j, i: (i, COL_GB * nj + j)),
        ],
        out_specs=pl.BlockSpec((MIX_TM, MIX_TN), lambda j, i: (i, j)),
        out_shape=jax.ShapeDtypeStruct((T, D_MODEL), BF16),
        compiler_params=_cparams(("arbitrary", "arbitrary")),
        name="mix",
    )(o_p, y_p, o_s, y_s, w_attn16, w_rnn16, z, z)


OUT_TM = 512


def _out_ln_kernel(m_ref, w_ref, x_ref, g_ref, b_ref, o32_ref, o16_ref):
    y = _layer_norm(ALPHA * x_ref[...] + _dot(m_ref[...], w_ref[...]), g_ref[...], b_ref[...])
    o32_ref[...] = y
    o16_ref[...] = y.astype(BF16)


def _out_ln(mixed, w_out16, x32, g, b, layer):
    row = pl.BlockSpec((None, 1, D_MODEL), lambda i: (layer, 0, 0))
    return pl.pallas_call(
        _out_ln_kernel,
        grid=(T // OUT_TM,),
        in_specs=[
            pl.BlockSpec((OUT_TM, D_MODEL), lambda i: (i, 0)),
            pl.BlockSpec((None, D_MODEL, D_MODEL), lambda i: (layer, 0, 0)),
            pl.BlockSpec((OUT_TM, D_MODEL), lambda i: (i, 0)),
            row, row,
        ],
        out_specs=[pl.BlockSpec((OUT_TM, D_MODEL), lambda i: (i, 0)),
                   pl.BlockSpec((OUT_TM, D_MODEL), lambda i: (i, 0))],
        out_shape=[jax.ShapeDtypeStruct((T, D_MODEL), F32), jax.ShapeDtypeStruct((T, D_MODEL), BF16)],
        compiler_params=_cparams(("arbitrary",)),
        name="out_ln",
    )(mixed, w_out16, x32, g, b)


RT_TM = 512
META_ROWS = SUBLANES


def _router_kernel(x_ref, w_ref, b_ref, meta_ref, wts_ref, cnt_ref, run_ref):
    i = pl.program_id(0)

    @pl.when(i == 0)
    def _():
        run_ref[...] = jnp.zeros(run_ref.shape, F32)

    logits = lax.dot_general(w_ref[...], x_ref[...], (((1,), (1,)), ((), ())),
                             precision=lax.Precision.HIGHEST, preferred_element_type=F32) + b_ref[...]
    mx = jnp.max(logits, axis=0, keepdims=True)
    e = jnp.exp(logits - mx)
    probs = e / jnp.sum(e, axis=0, keepdims=True)
    rows = [probs[j:j + 1, :] for j in range(N_EXPERTS)]

    best_score = None
    for g in range(N_GROUPS):
        v = rows[g * EXPERTS_PER_GROUP:(g + 1) * EXPERTS_PER_GROUP]
        m1 = functools.reduce(jnp.maximum, v)
        i1 = jnp.full(m1.shape, EXPERTS_PER_GROUP - 1, jnp.int32)
        for j in reversed(range(EXPERTS_PER_GROUP)):
            i1 = jnp.where(v[j] == m1, j, i1)
        rest = [jnp.where(i1 == j, -1.0, v[j]) for j in range(EXPERTS_PER_GROUP)]
        m2 = functools.reduce(jnp.maximum, rest)
        i2 = jnp.full(m1.shape, EXPERTS_PER_GROUP - 1, jnp.int32)
        for j in reversed(range(EXPERTS_PER_GROUP)):
            i2 = jnp.where(rest[j] == m2, j, i2)
        score = m1 + m2
        if best_score is None:
            best_score, b_m1, b_m2, b_e1, b_e2 = score, m1, m2, i1, i2
        else:
            better = score > best_score
            best_score = jnp.where(better, score, best_score)
            b_m1 = jnp.where(better, m1, b_m1)
            b_m2 = jnp.where(better, m2, b_m2)
            b_e1 = jnp.where(better, g * EXPERTS_PER_GROUP + i1, b_e1)
            b_e2 = jnp.where(better, g * EXPERTS_PER_GROUP + i2, b_e2)
    denom = b_m1 + b_m2
    w1 = b_m1 / denom
    w2 = b_m2 / denom

    eid = lax.broadcasted_iota(jnp.int32, (N_EXPERTS, RT_TM), 0)
    oh1 = eid == b_e1
    oh2 = eid == b_e2
    member = jnp.where(oh1 | oh2, 1.0, 0.0)
    earlier = jnp.where(lax.broadcasted_iota(jnp.int32, (RT_TM, RT_TM), 0)
                        < lax.broadcasted_iota(jnp.int32, (RT_TM, RT_TM), 1), 1.0, 0.0).astype(BF16)
    prefix = _dot(member.astype(BF16), earlier) + run_ref[:, :1]
    rank1 = jnp.sum(jnp.where(oh1, prefix, 0.0), axis=0, keepdims=True)
    rank2 = jnp.sum(jnp.where(oh2, prefix, 0.0), axis=0, keepdims=True)
    run_ref[...] = run_ref[...] + jnp.sum(member, axis=1, keepdims=True)
    cnt_ref[...] = run_ref[...]

    msub = lax.broadcasted_iota(jnp.int32, (META_ROWS, RT_TM), 0)
    meta_ref[...] = (jnp.where(msub == 0, b_e1, 0) + jnp.where(msub == 1, b_e2, 0)
                     + jnp.where(msub == 2, rank1.astype(jnp.int32), 0)
                     + jnp.where(msub == 3, rank2.astype(jnp.int32), 0))
    sub = lax.broadcasted_iota(jnp.int32, (LANES, RT_TM), 0)
    wts_t = jnp.where(sub == 0, w1, 0.0) + jnp.where(sub == 1, w2, 0.0)
    wts_ref[...] = wts_t.T


def _router(x32, router_w_t, router_b_col):
    return pl.pallas_call(
        _router_kernel,
        grid=(T // RT_TM,),
        in_specs=[
            pl.BlockSpec((RT_TM, D_MODEL), lambda i: (i, 0)),
            pl.BlockSpec((N_EXPERTS, D_MODEL), lambda i: (0, 0)),
            pl.BlockSpec((N_EXPERTS, 1), lambda i: (0, 0)),
        ],
        out_specs=[pl.BlockSpec((META_ROWS, RT_TM), lambda i: (0, i)),
                   pl.BlockSpec((RT_TM, LANES), lambda i: (i, 0)),
                   pl.BlockSpec((N_EXPERTS, LANES), lambda i: (0, 0))],
        out_shape=[jax.ShapeDtypeStruct((META_ROWS, T), jnp.int32),
                   jax.ShapeDtypeStruct((T, LANES), F32),
                   jax.ShapeDtypeStruct((N_EXPERTS, LANES), F32)],
        scratch_shapes=[pltpu.VMEM((N_EXPERTS, LANES), F32)],
        compiler_params=_cparams(("arbitrary",)),
        name="router",
    )(x32, router_w_t, router_b_col)


GATHER_UNROLL = 8


def _row_copy(src_hbm, src_row, dst_buf, slot, dst_row, sem):
    return pltpu.make_async_copy(src_hbm.at[pl.ds(src_row, 1), :], dst_buf.at[slot, pl.ds(dst_row, 1), :], sem.at[slot])


def _gather_start(idx_ref, base, n_rows, src_hbm, dst_buf, slot, sem):
    def body(r, carry):
        _row_copy(src_hbm, idx_ref[base + r], dst_buf, slot, r, sem).start()
        return carry

    lax.fori_loop(0, n_rows, body, 0, unroll=GATHER_UNROLL)


def _gather_start_inline(idx_ref, base, n_rows, src_hbm, dst_buf, slot, sem):
    for r in range(n_rows):
        _row_copy(src_hbm, idx_ref[base + r], dst_buf, slot, r, sem).start()


def _gather_wait(n_rows, src_hbm, dst_buf, slot, sem):
    def body(r, carry):
        _row_copy(src_hbm, 0, dst_buf, slot, r, sem).wait()
        return carry

    lax.fori_loop(0, n_rows, body, 0, unroll=GATHER_UNROLL)


MOE_TM = 256
MOE_NT = 2 * T // MOE_TM + N_EXPERTS
MOE_ROWS = MOE_NT * MOE_TM


def _moe_kernel(te_ref, nused_ref, src_ref, x_hbm, wg_ref, wu_ref, wd_ref, y_ref, buf, sem):
    i = pl.program_id(0)
    n_used = nused_ref[0]

    @pl.when(i == 0)
    def _():
        _gather_start(src_ref, 0, MOE_TM, x_hbm, buf, 0, sem)

    @pl.when(i + 1 < n_used)
    def _():
        _gather_start(src_ref, (i + 1) * MOE_TM, MOE_TM, x_hbm, buf, (i + 1) % 2, sem)

    @pl.when(i < n_used)
    def _():
        slot = i % 2
        _gather_wait(MOE_TM, x_hbm, buf, slot, sem)
        x = buf[slot].astype(BF16)
        he = (jax.nn.silu(_dot(x, wg_ref[...])) * _dot(x, wu_ref[...])).astype(BF16)
        y_ref[...] = _dot(he, wd_ref[...])

    @pl.when(i >= n_used)
    def _():
        y_ref[...] = jnp.zeros(y_ref.shape, F32)


def _moe(tile_expert, n_used, src_token, x32, wg16, wu16, wd16):
    grid_spec = pltpu.PrefetchScalarGridSpec(
        num_scalar_prefetch=3,
        grid=(MOE_NT,),
        in_specs=[
            pl.BlockSpec(memory_space=pl.ANY),
            pl.BlockSpec((None, D_MODEL, D_EXPERT), lambda i, te, nu, src: (te[i], 0, 0)),
            pl.BlockSpec((None, D_MODEL, D_EXPERT), lambda i, te, nu, src: (te[i], 0, 0)),
            pl.BlockSpec((None, D_EXPERT, D_MODEL), lambda i, te, nu, src: (te[i], 0, 0)),
        ],
        out_specs=pl.BlockSpec((MOE_TM, D_MODEL), lambda i, te, nu, src: (i, 0)),
        scratch_shapes=[pltpu.VMEM((2, MOE_TM, D_MODEL), F32), pltpu.SemaphoreType.DMA((2,))],
    )
    return pl.pallas_call(
        _moe_kernel,
        grid_spec=grid_spec,
        out_shape=jax.ShapeDtypeStruct((MOE_ROWS, D_MODEL), F32),
        compiler_params=_cparams(("arbitrary",)),
        name="moe",
    )(tile_expert, n_used, src_token, x32, wg16, wu16, wd16)


CMB_TM = 256
CMB_NT = T // CMB_TM


def _combine_kernel(pos_ref, y_hbm, x16_ref, x32_ref, wts_ref, p_ref, wpg_ref, wpp_ref, g_ref, b_ref,
                    o32_ref, o16_ref, buf, sem):
    i = pl.program_id(0)
    rows = 2 * CMB_TM

    @pl.when(i == 0)
    def _():
        _gather_start(pos_ref, 0, rows, y_hbm, buf, 0, sem)

    slot = i % 2
    _gather_wait(rows, y_hbm, buf, slot, sem)
    nxt = jnp.minimum(i + 1, CMB_NT - 1)
    _gather_start_inline(pos_ref, nxt * rows, rows, y_hbm, buf, 1 - slot, sem)
    wts = wts_ref[...]
    moe = wts[:, 0:1] * buf[slot, :CMB_TM, :] + wts[:, 1:2] * buf[slot, CMB_TM:, :]
    ple = jax.nn.sigmoid(_dot(x16_ref[...], wpg_ref[...])) * _dot(p_ref[...], wpp_ref[...])
    y = _layer_norm(ALPHA * x32_ref[...] + moe + ple, g_ref[...], b_ref[...])
    o32_ref[...] = y
    o16_ref[...] = y.astype(BF16)

    @pl.when(i == CMB_NT - 1)
    def _():
        _gather_wait(rows, y_hbm, buf, 1 - slot, sem)


def _combine(pos, y_sorted, x16, x32, wts, p16, w_gate16, w_proj16, g, b, layer):
    row = pl.BlockSpec((None, 1, D_MODEL), lambda i, pos: (layer, 0, 0))
    tile = pl.BlockSpec((CMB_TM, D_MODEL), lambda i, pos: (i, 0))
    grid_spec = pltpu.PrefetchScalarGridSpec(
        num_scalar_prefetch=1,
        grid=(CMB_NT,),
        in_specs=[
            pl.BlockSpec(memory_space=pl.ANY),
            tile, tile,
            pl.BlockSpec((CMB_TM, LANES), lambda i, pos: (i, 0)),
            pl.BlockSpec((None, CMB_TM, PLE_DIM), lambda i, pos: (layer, i, 0)),
            pl.BlockSpec((None, D_MODEL, D_MODEL), lambda i, pos: (layer, 0, 0)),
            pl.BlockSpec((None, PLE_DIM, D_MODEL), lambda i, pos: (layer, 0, 0)),
            row, row,
        ],
        out_specs=[tile, tile],
        scratch_shapes=[pltpu.VMEM((2, 2 * CMB_TM, D_MODEL), F32), pltpu.SemaphoreType.DMA((2,))],
    )
    return pl.pallas_call(
        _combine_kernel,
        grid_spec=grid_spec,
        out_shape=[jax.ShapeDtypeStruct((T, D_MODEL), F32), jax.ShapeDtypeStruct((T, D_MODEL), BF16)],
        compiler_params=_cparams(("arbitrary",)),
        name="combine",
    )(pos, y_sorted, x16, x32, wts, p16, w_gate16, w_proj16, g, b)


def _moe_schedule(meta, cnt):
    e1, e2, r1, r2 = meta[0], meta[1], meta[2], meta[3]
    counts = cnt[:, 0].astype(jnp.int32)
    tiles = (counts + MOE_TM - 1) // MOE_TM
    tile_end = jnp.cumsum(tiles)
    offset = (tile_end - tiles) * MOE_TM
    n_used = tile_end[-1:]
    tile_ids = jnp.arange(MOE_NT, dtype=jnp.int32)
    tile_expert = jnp.minimum(jnp.sum((tile_ids[:, None] >= tile_end[None, :]).astype(jnp.int32), axis=1),
                              N_EXPERTS - 1)
    pos1 = offset[e1] + r1
    pos2 = offset[e2] + r2
    tok = jnp.arange(T, dtype=jnp.int32)
    src_token = jnp.zeros((MOE_ROWS,), jnp.int32).at[jnp.concatenate([pos1, pos2])].set(jnp.concatenate([tok, tok]))
    pos = jnp.stack([pos1.reshape(CMB_NT, CMB_TM), pos2.reshape(CMB_NT, CMB_TM)], axis=1).reshape(-1)
    return tile_expert, n_used, src_token, pos


def kernel(x_prompt, x_sample, cache_k, cache_v, state_conv, state_h, page_table, p_prompt, p_sample,
           emb_ln_g, emb_ln_b, w_in, lambda_q1, lambda_k1, lambda_q2, lambda_k2, subln_w,
           conv_w, conv_b, rg_w_a, rg_b_a, rg_w_x, rg_b_x, rg_lambda,
           w_branch_attn, w_branch_rnn, w_out, ln1_g, ln1_b, router_w, router_b,
           moe_w_gate, moe_w_up, moe_w_down, w_ple_gate, w_ple_proj, ln2_g, ln2_b):
    row = lambda a: a.reshape(1, -1)
    rows3 = lambda a: a.reshape(DEPTH, 1, -1)

    w_attn16 = w_branch_attn.astype(BF16)
    w_rnn16 = w_branch_rnn.astype(BF16)
    w_out16 = w_out.astype(BF16)
    w_pg16 = w_ple_gate.astype(BF16)
    w_pp16 = w_ple_proj.astype(BF16)
    rg_wa16 = rg_w_a.astype(BF16)
    rg_wx16 = rg_w_x.astype(BF16)
    router_w_t = router_w.T
    router_b_col = router_b.reshape(N_EXPERTS, 1)
    ln1_g3, ln1_b3, ln2_g3, ln2_b3 = rows3(ln1_g), rows3(ln1_b), rows3(ln2_g), rows3(ln2_b)
    p16 = jnp.concatenate([p_prompt.reshape(DEPTH, TP, PLE_DIM), p_sample.reshape(DEPTH, TS, PLE_DIM)],
                          axis=1).astype(BF16)

    x32, x16 = _embed_ln(x_prompt.reshape(TP, D_MODEL), x_sample.reshape(TS, D_MODEL), row(emb_ln_g), row(emb_ln_b))

    outs = {k: [] for k in ("kp", "vp", "cp", "hp", "ks", "vs", "cs", "hs")}
    for l in range(DEPTH):
        lam_init = 0.8 - 0.6 * math.exp(-0.3 * l)
        lam_vecs = (row(lambda_q1[l]), row(lambda_k1[l]), row(lambda_q2[l]), row(lambda_k2[l]))
        subln = row(subln_w[l])
        rgw = (conv_w[l], row(conv_b[l]), rg_wa16[l], row(rg_b_a[l]), rg_wx16[l], row(rg_b_x[l]), row(rg_lambda[l]))

        z, wg16, wu16, wd16 = _in_proj(x16, w_in, moe_w_gate, moe_w_up, moe_w_down, l)

        kv = z[:, COL_K * D_MODEL:(COL_V + 1) * D_MODEL]
        xr = z[:, COL_XR * D_MODEL:(COL_XR + 1) * D_MODEL]
        k_s = kv[TP:, :D_MODEL].reshape(DEC_BATCH, DEC_SEQ, N_HEADS, HEAD_W)
        v_s = kv[TP:, D_MODEL:].reshape(DEC_BATCH, DEC_SEQ, N_HEADS, HEAD_W)

        o_p = _attn_prompt(z, lam_vecs, subln, lam_init)
        o_s = _attn_sample(z, k_s, v_s, cache_k, cache_v, page_table, lam_vecs, subln, lam_init, l)

        y_p, h_p = _rglru_prompt(z, rgw)
        hist = jnp.pad(state_conv[l], ((0, 0), (SUBLANES - (CONV_W - 1), 0), (0, 0))).reshape(TS, D_RNN)
        h0 = jnp.repeat(state_h[l], DEC_SEQ, axis=0)
        y_s, hs_s = _rglru_sample(z, hist, h0, rgw)

        mixed = _mix(o_p, y_p, o_s.astype(BF16), y_s, w_attn16, w_rnn16, z, l)
        x32, x16 = _out_ln(mixed, w_out16, x32, ln1_g3, ln1_b3, l)
        meta, wts, cnt = _router(x32, router_w_t, router_b_col)
        tile_expert, n_used, src_token, pos = _moe_schedule(meta, cnt)
        y_sorted = _moe(tile_expert, n_used, src_token, x32, wg16, wu16, wd16)
        x32, x16 = _combine(pos, y_sorted, x16, x32, wts, p16, w_pg16, w_pp16, ln2_g3, ln2_b3, l)

        outs["kp"].append(kv[:TP, :D_MODEL].reshape(BATCH, SEQ, N_HEADS, HEAD_W))
        outs["vp"].append(kv[:TP, D_MODEL:].reshape(BATCH, SEQ, N_HEADS, HEAD_W))
        outs["ks"].append(k_s)
        outs["vs"].append(v_s)
        outs["cp"].append(xr[:TP].reshape(BATCH, SEQ, D_RNN)[:, SEQ - (CONV_W - 1):])
        outs["cs"].append(xr[TP:].reshape(DEC_BATCH, DEC_SEQ, D_RNN)[:, DEC_SEQ - (CONV_W - 1):])
        outs["hp"].append(h_p.reshape(BATCH, D_RNN))
        outs["hs"].append(hs_s.reshape(DEC_BATCH, DEC_SEQ, D_RNN)[:, DEC_SEQ - 1])

    st = {k: jnp.stack(v) for k, v in outs.items()}
    return (x32[:TP].reshape(BATCH, SEQ, D_MODEL), x32[TP:].reshape(DEC_BATCH, DEC_SEQ, D_MODEL),
            st["kp"], st["vp"], st["cp"], st["hp"], st["ks"], st["vs"], st["cs"], st["hs"])
```
